```python
import jax, jax.numpy as jnp
from jax import lax
import numpy as np

D_MODEL = 1024
BATCH = 8
SEQ = 4096
DEPTH = 4

GRID_W = 64
CTX_LEN = 256
N_MIXERS = 4
ALPHA = (2 * DEPTH) ** 0.25
BETA = (8 * DEPTH) ** -0.25
LN_EPS = 1e-5
RMS_EPS = 1e-6
NEG_INF = -1e30
ROPE_THETA = 10000.0
Q_BLOCK = 128
F32 = jnp.float32

HGRN_HEADS = 8
HGRN_DK = D_MODEL // HGRN_HEADS
HGRN_DV = D_MODEL // HGRN_HEADS
GLA_CHUNK = 64

SWA_Q_HEADS = 16
SWA_KV_HEADS = 4
SWA_HEAD_DIM = D_MODEL // SWA_Q_HEADS
SWA_WINDOW = 128

NA_HEADS = 16
NA_HEAD_DIM = D_MODEL // NA_HEADS
NA_WIN_ROWS = 8
NA_WIN_COLS = 16

GQA_Q_HEADS = 8
GQA_KV_HEADS = 2
GQA_HEAD_DIM = D_MODEL // GQA_Q_HEADS

PEER_HEADS = 8
PEER_NKEYS = 128
PEER_N = PEER_NKEYS * PEER_NKEYS
PEER_DQ = 256
PEER_TOPK = 16
PEER_CHUNK = 128

kernel_name = 'hybrid_diffusion_hgrn2_swa_natten_gqa_peer'


def layer_norm(x, g, b):
    xf = x.astype(F32)
    mu = jnp.mean(xf, axis=-1, keepdims=True)
    var = jnp.mean(jnp.square(xf - mu), axis=-1, keepdims=True)
    return ((xf - mu) * lax.rsqrt(var + LN_EPS) * g + b).astype(x.dtype)


def rms_norm(x, g):
    xf = x.astype(F32)
    return (xf * lax.rsqrt(jnp.mean(jnp.square(xf), axis=-1, keepdims=True) + RMS_EPS) * g).astype(x.dtype)


def axial_rope(x):
    n_tok, dh = x.shape[1], x.shape[-1]
    t = jnp.arange(n_tok)
    pos = jnp.stack([t // GRID_W, t % GRID_W]).astype(F32)
    d_axis = dh // 2
    inv = ROPE_THETA ** (-jnp.arange(0, d_axis, 2, dtype=F32) / d_axis)
    ang = pos[:, :, None] * inv
    cos = jnp.cos(ang)[:, :, None, :].astype(x.dtype)
    sin = jnp.sin(ang)[:, :, None, :].astype(x.dtype)
    parts = []
    for a, xa in enumerate(jnp.split(x, 2, axis=-1)):
        x1, x2 = jnp.split(xa, 2, axis=-1)
        parts.append(jnp.concatenate([x1 * cos[a] - x2 * sin[a], x2 * cos[a] + x1 * sin[a]], axis=-1))
    return jnp.concatenate(parts, axis=-1)


def qkv_heads(h, w_in, hq, hkv, dh):
    z = h @ w_in
    bsz, n_tok, _ = z.shape
    q = z[..., :hq * dh].reshape(bsz, n_tok, hq, dh)
    k = z[..., hq * dh:(hq + hkv) * dh].reshape(bsz, n_tok, hkv, dh)
    v = z[..., (hq + hkv) * dh:].reshape(bsz, n_tok, hkv, dh)
    return q, k, v


def attend(q, k, v, bias=None, sink=None):
    bsz, lq, hq, dh = q.shape
    hkv = k.shape[2]
    grp = hq // hkv
    s = jnp.einsum('bqhgd,bkhd->bhgqk', q.reshape(bsz, lq, hkv, grp, dh), k).astype(F32) * (dh ** -0.5)
    if bias is not None:
        s = s + bias
    if sink is None:
        p = jax.nn.softmax(s, axis=-1)
    else:
        sk = jnp.broadcast_to(sink.astype(F32).reshape(1, hkv, grp, 1, 1), s.shape[:-1] + (1,))
        p = jax.nn.softmax(jnp.concatenate([s, sk], axis=-1), axis=-1)[..., :-1]
    o = jnp.einsum('bhgqk,bkhd->bqhgd', p.astype(v.dtype), v)
    return o.reshape(bsz, lq, hq, dh)


def unblock(o):
    nb, bsz, blk, h, dh = o.shape
    return o.transpose(1, 0, 2, 3, 4).reshape(bsz, nb * blk, h * dh)


def gla_chunkwise(q, k, v, logf, s0):
    bsz, h, n_tok, _ = q.shape
    n_chunks = n_tok // GLA_CHUNK
    tri = jnp.tril(jnp.ones((GLA_CHUNK, GLA_CHUNK), bool))

    def chunks(a):
        return a.reshape(bsz, h, n_chunks, GLA_CHUNK, a.shape[-1]).transpose(2, 0, 1, 3, 4)

    def step(state, inp):
        qb, kb, vb, gb = inp
        b = jnp.cumsum(gb, axis=2)
        rel = jnp.where(tri[:, :, None], b[:, :, :, None, :] - b[:, :, None, :, :], -jnp.inf)
        att = jnp.einsum('bhtk,bhsk,bhtsk->bhts', qb, kb, jnp.exp(rel))
        o = jnp.einsum('bhts,bhsv->bhtv', att, vb) + jnp.einsum('bhtk,bhkv->bhtv', qb * jnp.exp(b), state)
        b_end = b[:, :, -1:, :]
        state = jnp.exp(b_end[:, :, 0, :, None]) * state + jnp.einsum('bhsk,bhsv->bhkv', kb * jnp.exp(b_end - b), vb)
        return state, o

    state, o = lax.scan(step, s0, (chunks(q), chunks(k), chunks(v), chunks(logf)))
    return o.transpose(1, 2, 0, 3, 4).reshape(bsz, h, n_tok, -1), state


def hgrn2_mixer(hx, hc, w_in, lb, norm_g, w_out, with_ctx):
    def gates(h):
        bsz, n_tok, _ = h.shape
        q, i, f_fw, f_bw, g = jnp.split(h @ w_in, 5, axis=-1)
        heads = lambda a: a.reshape(bsz, n_tok, HGRN_HEADS, -1).transpose(0, 2, 1, 3).astype(F32)

        def decay(fz):
            f = lb + (1.0 - lb) * jax.nn.sigmoid(fz.astype(F32))
            return heads(1.0 - f), heads(jnp.log(f))
        return heads(jax.nn.silu(q)), heads(i), decay(f_fw), decay(f_bw), g

    def readout(o, g):
        bsz, _, n_tok, _ = o.shape
        o = rms_norm(o.transpose(0, 2, 1, 3), norm_g).reshape(bsz, n_tok, D_MODEL)
        return (o.astype(g.dtype) * jax.nn.silu(g)) @ w_out

    flip = lambda a: jnp.flip(a, axis=2)
    qc, vc, (kcf, lcf), (kcb, lcb), gc = gates(hc)
    s0 = jnp.zeros((hc.shape[0], HGRN_HEADS, HGRN_DK, HGRN_DV), F32)
    oc_f, sc_f = gla_chunkwise(qc, kcf, vc, lcf, s0)
    oc_b, sc_b = gla_chunkwise(flip(qc), flip(kcb), flip(vc), flip(lcb), s0)
    qx, vx, (kxf, lxf), (kxb, lxb), gx = gates(hx)
    ox_f, _ = gla_chunkwise(qx, kxf, vx, lxf, sc_f)
    ox_b, _ = gla_chunkwise(flip(qx), flip(kxb), flip(vx), flip(lxb), sc_b)
    yx = readout(ox_f + flip(ox_b), gx)
    yc = readout(oc_f + flip(oc_b), gc) if with_ctx else None
    return yx, yc


def swa_mixer(hx, hc, w_in, sink, w_out, with_ctx):
    bsz, s_len, _ = hx.shape
    qx, kx, vx = qkv_heads(hx, w_in, SWA_Q_HEADS, SWA_KV_HEADS, SWA_HEAD_DIM)
    qc, kc, vc = qkv_heads(hc, w_in, SWA_Q_HEADS, SWA_KV_HEADS, SWA_HEAD_DIM)
    qx, kx = axial_rope(qx), axial_rope(kx)
    pad = ((0, 0), (SWA_WINDOW, SWA_WINDOW), (0, 0), (0, 0))
    kp, vp = jnp.pad(kx, pad), jnp.pad(vx, pad)
    span = Q_BLOCK + 2 * SWA_WINDOW
    ctx_bias = jnp.zeros((Q_BLOCK, kc.shape[1]), F32)

    def block(n):
        start = n * Q_BLOCK
        q = lax.dynamic_slice_in_dim(qx, start, Q_BLOCK, axis=1)
        k = lax.dynamic_slice_in_dim(kp, start, span, axis=1)
        v = lax.dynamic_slice_in_dim(vp, start, span, axis=1)
        qpos = start + jnp.arange(Q_BLOCK)
        kpos = start - SWA_WINDOW + jnp.arange(span)
        ok = (jnp.abs(qpos[:, None] - kpos[None, :]) <= SWA_WINDOW) & (kpos[None, :] >= 0) & (kpos[None, :] < s_len)
        bias = jnp.concatenate([jnp.where(ok, 0.0, NEG_INF), ctx_bias], axis=-1)
        return attend(q, jnp.concatenate([k, kc], axis=1), jnp.concatenate([v, vc], axis=1), bias, sink)

    yx = unblock(lax.map(block, jnp.arange(s_len // Q_BLOCK))) @ w_out
    yc = attend(qc, kc, vc, sink=sink).reshape(bsz, hc.shape[1], D_MODEL) @ w_out if with_ctx else None
    return yx, yc


def na_mixer(hx, hc, w_in, rpb, w_out, with_ctx):
    bsz, s_len, _ = hx.shape
    rows = s_len // GRID_W
    wr = min(NA_WIN_ROWS, rows)
    qx, kx, vx = qkv_heads(hx, w_in, NA_HEADS, NA_HEADS, NA_HEAD_DIM)
    qc, kc, vc = qkv_heads(hc, w_in, NA_HEADS, NA_HEADS, NA_HEAD_DIM)
    col = jnp.arange(GRID_W)
    cstart = jnp.clip(col - NA_WIN_COLS // 2, 0, GRID_W - NA_WIN_COLS)
    col_ok = (col[None, :] >= cstart[:, None]) & (col[None, :] < cstart[:, None] + NA_WIN_COLS)
    dcol = jnp.clip(col[None, :] - col[:, None] + NA_WIN_COLS - 1, 0, 2 * NA_WIN_COLS - 2)
    ctx_bias = jnp.zeros((NA_HEADS, GRID_W, kc.shape[1]), F32)

    def row_block(r):
        rstart = jnp.clip(r - wr // 2, 0, rows - wr)
        q = lax.dynamic_slice_in_dim(qx, r * GRID_W, GRID_W, axis=1)
        k = lax.dynamic_slice_in_dim(kx, rstart * GRID_W, wr * GRID_W, axis=1)
        v = lax.dynamic_slice_in_dim(vx, rstart * GRID_W, wr * GRID_W, axis=1)
        drow = rstart + jnp.arange(wr) - r + NA_WIN_ROWS - 1
        rel = rpb[:, drow[None, :, None], dcol[:, None, :]].astype(F32)
        bias = jnp.where(col_ok[:, None, :], rel, NEG_INF).reshape(NA_HEADS, GRID_W, wr * GRID_W)
        bias = jnp.concatenate([bias, ctx_bias], axis=-1)[None, :, None]
        return attend(q, jnp.concatenate([k, kc], axis=1), jnp.concatenate([v, vc], axis=1), bias)

    yx = unblock(lax.map(row_block, jnp.arange(rows))) @ w_out
    yc = attend(qc, kc, vc).reshape(bsz, hc.shape[1], D_MODEL) @ w_out if with_ctx else None
    return yx, yc


def gqa_mixer(hx, hc, w_in, q_norm, k_norm, w_out, with_ctx):
    bsz, s_len, _ = hx.shape
    qx, kx, vx = qkv_heads(hx, w_in, GQA_Q_HEADS, GQA_KV_HEADS, GQA_HEAD_DIM)
    qc, kc, vc = qkv_heads(hc, w_in, GQA_Q_HEADS, GQA_KV_HEADS, GQA_HEAD_DIM)
    qx, kx = axial_rope(rms_norm(qx, q_norm)), axial_rope(rms_norm(kx, k_norm))
    qc, kc = rms_norm(qc, q_norm), rms_norm(kc, k_norm)
    k_all = jnp.concatenate([kx, kc], axis=1)
    v_all = jnp.concatenate([vx, vc], axis=1)

    def block(n):
        return attend(lax.dynamic_slice_in_dim(qx, n * Q_BLOCK, Q_BLOCK, axis=1), k_all, v_all)

    yx = unblock(lax.map(block, jnp.arange(s_len // Q_BLOCK))) @ w_out
    yc = attend(qc, kc, vc).reshape(bsz, hc.shape[1], D_MODEL) @ w_out if with_ctx else None
    return yx, yc


def peer_ffn(tokens, wq, sub_keys, u, v):
    n_tok, d = tokens.shape

    def chunk(h):
        t = h.shape[0]
        q = (h @ wq).reshape(t, PEER_HEADS, 2, PEER_DQ // 2)
        s = jnp.einsum('thpd,hpnd->thpn', q, sub_keys).astype(F32)
        sv, si = lax.top_k(s, PEER_TOPK)
        cand_s = (sv[:, :, 0, :, None] + sv[:, :, 1, None, :]).reshape(t, PEER_HEADS, -1)
        cand_i = (si[:, :, 0, :, None] * PEER_NKEYS + si[:, :, 1, None, :]).reshape(t, PEER_HEADS, -1)
        best_s, best = lax.top_k(cand_s, PEER_TOPK)
        idx = jnp.take_along_axis(cand_i, best, axis=-1)
        gate = jax.nn.softmax(best_s, axis=-1)
        act = jax.nn.gelu(jnp.einsum('td,thkd->thk', h, u[idx]).astype(F32))
        return jnp.einsum('thk,thkd->td', (gate * act).astype(v.dtype), v[idx])

    out = lax.map(chunk, tokens.reshape(n_tok // PEER_CHUNK, PEER_CHUNK, d))
    return out.reshape(n_tok, d)


def setup_inputs(seed: int = 0) -> dict:
    key = jax.random.key(seed)
    ks = iter(jax.random.split(key, 32))
    nrm = lambda shape, scale: jax.random.normal(next(ks), shape, F32) * scale
    occ = [len(range(m, DEPTH, N_MIXERS)) for m in range(N_MIXERS)]
    d = D_MODEL
    out_scale = d ** -0.5 * BETA
    return {
        'x': nrm((BATCH, SEQ, d), 1.0),
        'c': nrm((BATCH, d), 1.0),
        'ctx': nrm((BATCH, CTX_LEN, d), 1.0),
        'c_ctx': nrm((d,), 1.0),
        'mod_w': nrm((DEPTH, d, 6 * d), d ** -0.5),
        'mod_b': nrm((DEPTH, 6 * d), 0.02),
        'ln_g': 1.0 + nrm((DEPTH, 2, d), 0.02),
        'ln_b': nrm((DEPTH, 2, d), 0.02),
        'peer_wq': nrm((DEPTH, d, PEER_HEADS * PEER_DQ), d ** -0.5),
        'peer_keys': nrm((DEPTH, PEER_HEADS, 2, PEER_NKEYS, PEER_DQ // 2), (PEER_DQ // 2) ** -0.5),
        'peer_u': nrm((DEPTH, PEER_N, d), d ** -0.5),
        'peer_v': nrm((DEPTH, PEER_N, d), BETA),
        'hgrn_w_in': nrm((occ[0], d, 5 * d), d ** -0.5),
        'hgrn_lb_logits': nrm((DEPTH + 1, HGRN_HEADS * HGRN_DK), 0.1),
        'hgrn_norm_g': 1.0 + nrm((occ[0], HGRN_DV), 0.02),
        'hgrn_w_out': nrm((occ[0], d, d), out_scale),
        'swa_w_in': nrm((occ[1], d, (SWA_Q_HEADS + 2 * SWA_KV_HEADS) * SWA_HEAD_DIM), d ** -0.5),
        'swa_sink': nrm((occ[1], SWA_Q_HEADS), 0.1),
        'swa_w_out': nrm((occ[1], SWA_Q_HEADS * SWA_HEAD_DIM, d), out_scale),
        'na_w_in': nrm((occ[2], d, 3 * NA_HEADS * NA_HEAD_DIM), d ** -0.5),
        'na_rpb': nrm((occ[2], NA_HEADS, 2 * NA_WIN_ROWS - 1, 2 * NA_WIN_COLS - 1), 0.02),
        'na_w_out': nrm((occ[2], NA_HEADS * NA_HEAD_DIM, d), out_scale),
        'gqa_w_in': nrm((occ[3], d, (GQA_Q_HEADS + 2 * GQA_KV_HEADS) * GQA_HEAD_DIM), d ** -0.5),
        'gqa_q_norm': 1.0 + nrm((occ[3], GQA_HEAD_DIM), 0.02),
        'gqa_k_norm': 1.0 + nrm((occ[3], GQA_HEAD_DIM), 0.02),
        'gqa_w_out': nrm((occ[3], GQA_Q_HEADS * GQA_HEAD_DIM, d), out_scale),
    }


def reference(x, c, ctx, c_ctx, mod_w, mod_b, ln_g, ln_b, peer_wq, peer_keys, peer_u, peer_v,
              hgrn_w_in, hgrn_lb_logits, hgrn_norm_g, hgrn_w_out,
              swa_w_in, swa_sink, swa_w_out,
              na_w_in, na_rpb, na_w_out,
              gqa_w_in, gqa_q_norm, gqa_k_norm, gqa_w_out):
    lower_bounds = jnp.cumsum(jax.nn.softmax(hgrn_lb_logits.astype(F32), axis=0), axis=0)
    cond_x = jax.nn.silu(c)
    cond_c = jax.nn.silu(c_ctx)
    for i in range(DEPTH):
        kind, occ = i % N_MIXERS, i // N_MIXERS
        last = i == DEPTH - 1
        mx = jnp.split((cond_x @ mod_w[i] + mod_b[i])[:, None, :], 6, axis=-1)
        mc = jnp.split(cond_c @ mod_w[i] + mod_b[i], 6, axis=-1)
        hx = x * (1.0 + mx[1]) + mx[0]
        hc = ctx * (1.0 + mc[1]) + mc[0]
        if kind == 0:
            yx, yc = hgrn2_mixer(hx, hc, hgrn_w_in[occ], lower_bounds[i], hgrn_norm_g[occ], hgrn_w_out[occ], not last)
        elif kind == 1:
            yx, yc = swa_mixer(hx, hc, swa_w_in[occ], swa_sink[occ], swa_w_out[occ], not last)
        elif kind == 2:
            yx, yc = na_mixer(hx, hc, na_w_in[occ], na_rpb[occ], na_w_out[occ], not last)
        else:
            yx, yc = gqa_mixer(hx, hc, gqa_w_in[occ], gqa_q_norm[occ], gqa_k_norm[occ], gqa_w_out[occ], not last)
        x = layer_norm(ALPHA * x + mx[2] * yx, ln_g[i, 0], ln_b[i, 0])
        tokens = (x * (1.0 + mx[4]) + mx[3]).reshape(-1, D_MODEL)
        if not last:
            ctx = layer_norm(ALPHA * ctx + mc[2] * yc, ln_g[i, 0], ln_b[i, 0])
            tokens = jnp.concatenate([tokens, (ctx * (1.0 + mc[4]) + mc[3]).reshape(-1, D_MODEL)], axis=0)
        y = peer_ffn(tokens, peer_wq[i], peer_keys[i], peer_u[i], peer_v[i])
        n_lat = x.shape[0] * x.shape[1]
        x = layer_norm(ALPHA * x + mx[5] * y[:n_lat].reshape(x.shape), ln_g[i, 1], ln_b[i, 1])
        if not last:
            ctx = layer_norm(ALPHA * ctx + mc[5] * y[n_lat:].reshape(ctx.shape), ln_g[i, 1], ln_b[i, 1])
    return x
```

```python
import functools

import jax
import jax.numpy as jnp
from jax import lax
from jax.experimental import pallas as pl
from jax.experimental.pallas import tpu as pltpu

F32 = jnp.float32
BF16 = jnp.bfloat16

LANES = 128
GRID_W = 64
LN_EPS = 1e-5
RMS_EPS = 1e-6
NEG_INF = -1e30
ROPE_THETA = 10000.0

HGRN_HEADS = 8
GLA_CHUNK = 64
GLA_LEVELS = 6

SWA_Q_HEADS = 16
SWA_KV_HEADS = 4
SWA_WINDOW = 128
SWA_QBLK = 128

NA_HEADS = 16
NA_WIN_ROWS = 8
NA_WIN_COLS = 16
NA_ROWS_PER_STEP = 4

GQA_Q_HEADS = 8
GQA_KV_HEADS = 2
GQA_QBLK = 128

PEER_HEADS = 8
PEER_NKEYS = 128
PEER_TOPK = 16
PEER_TOK_TILE = 256
PEER_IBLK = 8

VMEM_LIMIT = 56 * 1024 * 1024


def _cparams(*sem):
    return pltpu.CompilerParams(dimension_semantics=sem, vmem_limit_bytes=VMEM_LIMIT)


def _nt(a, b):
    return lax.dot_general(a, b, (((1,), (1,)), ((), ())), preferred_element_type=F32)


def _nn(a, b):
    return jnp.dot(a, b, preferred_element_type=F32)


def _sigmoid(x):
    return 1.0 / (1.0 + jnp.exp(-x))


def _silu(x):
    return x * _sigmoid(x)


def _layer_norm(u, g, b):
    mu = jnp.mean(u, axis=-1, keepdims=True)
    d = u - mu
    var = jnp.mean(d * d, axis=-1, keepdims=True)
    return d * lax.rsqrt(var + LN_EPS) * g + b


def _row_tile(seq, n_ctx_rows, pref):
    for t in (512, 256, 128, 64):
        if t <= pref and seq % t == 0 and n_ctx_rows % t == 0:
            return t
    raise ValueError("no row tile")


def _mod_row_map(tm, seq, n_batch):
    return lambda i: (jnp.minimum((i * tm) // seq, n_batch), 0, 0)


def _mod_kernel(c_ref, w_ref, b_ref, o_ref):
    h = _silu(c_ref[...]).astype(BF16)
    o_ref[0] = _nn(h, w_ref[0].astype(BF16)) + b_ref[0]


def _modulation(cond, mod_w, mod_b):
    depth, d, n = mod_w.shape
    tn = 1536
    return pl.pallas_call(
        _mod_kernel,
        grid=(depth, n // tn),
        in_specs=[pl.BlockSpec((16, d), lambda l, j: (0, 0)),
                  pl.BlockSpec((1, d, tn), lambda l, j: (l, 0, j)),
                  pl.BlockSpec((1, 1, tn), lambda l, j: (l, 0, j))],
        out_specs=pl.BlockSpec((1, 16, tn), lambda l, j: (l, 0, j)),
        out_shape=jax.ShapeDtypeStruct((depth, 16, n), F32),
        compiler_params=_cparams("parallel", "parallel"),
        name="modulation",
    )(cond, mod_w, mod_b.reshape(depth, 1, n))


def _rope_tables(seq, dh, tm):
    t = jnp.arange(seq)
    pos = jnp.stack([t // GRID_W, t % GRID_W]).astype(F32)
    d_axis = dh // 2
    inv = ROPE_THETA ** (-jnp.arange(0, d_axis, 2, dtype=F32) / d_axis)
    ang = pos[:, :, None] * inv
    cos, sin = jnp.cos(ang), jnp.sin(ang)
    cos_h = jnp.concatenate([cos[0], cos[0], cos[1], cos[1]], axis=-1)
    sin_h = jnp.concatenate([-sin[0], sin[0], -sin[1], sin[1]], axis=-1)
    reps = LANES // dh
    cos_t = jnp.concatenate([jnp.tile(cos_h, (1, reps)), jnp.ones((tm, LANES), F32)], axis=0)
    sin_t = jnp.concatenate([jnp.tile(sin_h, (1, reps)), jnp.zeros((tm, LANES), F32)], axis=0)
    return cos_t, sin_t


def _rope_block(zb, cos, sin, quarter):
    lane = lax.broadcasted_iota(jnp.int32, zb.shape, 1)
    first = (lane % (2 * quarter)) < quarter
    up = pltpu.roll(zb, LANES - quarter, 1)
    down = pltpu.roll(zb, quarter, 1)
    return zb * cos + jnp.where(first, up, down) * sin


def _dup_half(blk, half):
    lane = lax.broadcasted_iota(jnp.int32, blk.shape, 1)
    swapped = pltpu.roll(blk, 64, 1)
    keep = (lane < 64) if half == 0 else (lane >= 64)
    return jnp.where(keep, blk, swapped)


def _modulate(x_ref, mod_ref, shift_row):
    return x_ref[...] * (1.0 + mod_ref[0, shift_row + 1:shift_row + 2, :]) + mod_ref[0, shift_row:shift_row + 1, :]


def _in_hgrn_kernel(x_ref, mod_ref, w_ref, o_ref):
    h = _modulate(x_ref, mod_ref, 0).astype(BF16)
    o_ref[...] = _nn(h, w_ref[...])


def _in_swa_kernel(x_ref, mod_ref, w_ref, cos_ref, sin_ref, q_ref, k_ref, v_ref):
    h = _modulate(x_ref, mod_ref, 0).astype(BF16)
    z = _nn(h, w_ref[...])
    cos, sin = cos_ref[...], sin_ref[...]
    dq = SWA_Q_HEADS * 64
    dkv = SWA_KV_HEADS * 64
    for blk in range(dq // LANES):
        zb = z[:, blk * LANES:(blk + 1) * LANES]
        q_ref[:, blk * LANES:(blk + 1) * LANES] = (_rope_block(zb, cos, sin, 16) * 0.125).astype(BF16)
    for blk in range(dkv // LANES):
        kb = _rope_block(z[:, dq + blk * LANES:dq + (blk + 1) * LANES], cos, sin, 16)
        vb = z[:, dq + dkv + blk * LANES:dq + dkv + (blk + 1) * LANES]
        for half in range(2):
            kv = 2 * blk + half
            k_ref[:, kv * LANES:(kv + 1) * LANES] = _dup_half(kb, half).astype(BF16)
            v_ref[:, kv * LANES:(kv + 1) * LANES] = _dup_half(vb, half).astype(BF16)


def _in_na_kernel(x_ref, mod_ref, w_ref, q_ref, k_ref, v_ref):
    h = _modulate(x_ref, mod_ref, 0).astype(BF16)
    z = _nn(h, w_ref[...])
    d = q_ref.shape[1]
    q_ref[...] = (z[:, :d] * 0.125).astype(BF16)
    k_ref[...] = z[:, d:2 * d].astype(BF16)
    v_ref[...] = z[:, 2 * d:].astype(BF16)


def _rms_block(zb, g):
    return zb * lax.rsqrt(jnp.mean(zb * zb, axis=-1, keepdims=True) + RMS_EPS) * g


def _in_gqa_kernel(x_ref, mod_ref, w_ref, cos_ref, sin_ref, qn_ref, kn_ref, q_ref, k_ref, v_ref):
    h = _modulate(x_ref, mod_ref, 0).astype(BF16)
    z = _nn(h, w_ref[...])
    cos, sin = cos_ref[...], sin_ref[...]
    dq = GQA_Q_HEADS * LANES
    dkv = GQA_KV_HEADS * LANES
    scale = LANES ** -0.5
    for hd in range(GQA_Q_HEADS):
        zb = _rms_block(z[:, hd * LANES:(hd + 1) * LANES], qn_ref[...])
        q_ref[:, hd * LANES:(hd + 1) * LANES] = (_rope_block(zb, cos, sin, 32) * scale).astype(BF16)
    for hd in range(GQA_KV_HEADS):
        zb = _rms_block(z[:, dq + hd * LANES:dq + (hd + 1) * LANES], kn_ref[...])
        k_ref[:, hd * LANES:(hd + 1) * LANES] = _rope_block(zb, cos, sin, 32).astype(BF16)
    v_ref[...] = z[:, dq + dkv:].astype(BF16)


def _in_proj(kernel, x, mod_l, w, extra, extra_specs, out_widths, out_dtype, tm, seq, n_batch, n_rows=None):
    t_all, d = x.shape
    n_rows = t_all if n_rows is None else n_rows
    n = w.shape[1]
    in_specs = [pl.BlockSpec((tm, d), lambda i: (i, 0)),
                pl.BlockSpec((1, 6, d), _mod_row_map(tm, seq, n_batch)),
                pl.BlockSpec((d, n), lambda i: (0, 0))] + extra_specs
    outs = [jax.ShapeDtypeStruct((n_rows, wd), out_dtype) for wd in out_widths]
    out_specs = [pl.BlockSpec((tm, wd), lambda i: (i, 0)) for wd in out_widths]
    single = len(outs) == 1
    res = pl.pallas_call(
        kernel,
        grid=(n_rows // tm,),
        in_specs=in_specs,
        out_specs=out_specs[0] if single else out_specs,
        out_shape=outs[0] if single else outs,
        compiler_params=_cparams("parallel"),
        name=kernel.__name__.strip("_"),
    )(x, mod_l, w, *extra)
    return res


def _rope_specs(tm, seq, n_lat):
    idx = lambda i: (jnp.where(i * tm < n_lat, (i * tm % seq) // tm, seq // tm), 0)
    return [pl.BlockSpec((tm, LANES), idx), pl.BlockSpec((tm, LANES), idx)]


def _softmax_pv(scores, values, extra_logit=None):
    m = scores[0].max(axis=-1, keepdims=True)
    for s in scores[1:]:
        m = jnp.maximum(m, s.max(axis=-1, keepdims=True))
    if extra_logit is not None:
        m = jnp.maximum(m, extra_logit)
    den = jnp.exp(extra_logit - m) if extra_logit is not None else 0.0
    acc = None
    for s, v in zip(scores, values):
        p = jnp.exp(s - m)
        den = den + p.sum(axis=-1, keepdims=True)
        o = _nn(p.astype(BF16), v)
        acc = o if acc is None else acc + o
    return acc / den


def _half_masked_rows(q, n_heads):
    m = q.shape[0]
    lane = lax.broadcasted_iota(jnp.int32, (m, LANES), 1)
    rows = []
    for j in range(n_heads):
        blk = q[:, (j // 2) * LANES:(j // 2 + 1) * LANES]
        keep = (lane < 64) if j % 2 == 0 else (lane >= 64)
        rows.append(jnp.where(keep, blk, jnp.zeros_like(blk)))
    return jnp.concatenate(rows, axis=0)


def _merge_halves(o, n_heads, m):
    lane = lax.broadcasted_iota(jnp.int32, (m, LANES), 1)
    outs = []
    for b in range(n_heads // 2):
        outs.append(jnp.where(lane < 64, o[(2 * b) * m:(2 * b + 1) * m], o[(2 * b + 1) * m:(2 * b + 2) * m]))
    return outs


def _sink_column(sink_ref, first_head, n_heads, m):
    return jnp.concatenate([jnp.full((m, 1), sink_ref[first_head + j], F32) for j in range(n_heads)], axis=0)


def _swa_x_kernel(sink_ref, q_ref, k_ref, v_ref, kc_ref, vc_ref, o_ref, *, seq):
    g, n = pl.program_id(1), pl.program_id(2)
    grp = SWA_Q_HEADS // SWA_KV_HEADS
    span = SWA_QBLK + 2 * SWA_WINDOW
    start = n * SWA_QBLK
    cstart = pl.multiple_of(jnp.clip(start - SWA_WINDOW, 0, seq - span), SWA_QBLK)
    q4 = _half_masked_rows(q_ref[...], grp)
    s_b = _nt(q4, k_ref[pl.ds(cstart, span), :])
    s_c = _nt(q4, kc_ref[...])
    qpos = start + lax.broadcasted_iota(jnp.int32, s_b.shape, 0) % SWA_QBLK
    kpos = cstart + lax.broadcasted_iota(jnp.int32, s_b.shape, 1)
    s_b = jnp.where(jnp.abs(qpos - kpos) <= SWA_WINDOW, s_b, NEG_INF)
    sk = _sink_column(sink_ref, g * grp, grp, SWA_QBLK)
    o = _softmax_pv([s_b, s_c], [v_ref[pl.ds(cstart, span), :], vc_ref[...]], sk)
    for b, ob in enumerate(_merge_halves(o, grp, SWA_QBLK)):
        o_ref[:, b * LANES:(b + 1) * LANES] = ob.astype(BF16)


def _swa_c_kernel(sink_ref, q_ref, kc_ref, vc_ref, prev_ref, o_ref):
    del prev_ref
    g = pl.program_id(1)
    grp = SWA_Q_HEADS // SWA_KV_HEADS
    m = q_ref.shape[0]
    q4 = _half_masked_rows(q_ref[...], grp)
    sk = _sink_column(sink_ref, g * grp, grp, m)
    o = _softmax_pv([_nt(q4, kc_ref[...])], [vc_ref[...]], sk)
    for b, ob in enumerate(_merge_halves(o, grp, m)):
        o_ref[:, b * LANES:(b + 1) * LANES] = ob.astype(BF16)


def _na_x_kernel(q_ref, k_ref, v_ref, kc_ref, vc_ref, bias_ref, o_ref, *, n_grid_rows):
    rblk = pl.program_id(2)
    win = NA_WIN_ROWS * GRID_W
    for rr in range(NA_ROWS_PER_STEP):
        r = rblk * NA_ROWS_PER_STEP + rr
        rstart = jnp.clip(r - NA_WIN_ROWS // 2, 0, n_grid_rows - NA_WIN_ROWS)
        var = rstart - r + NA_WIN_ROWS - 1
        kstart = pl.multiple_of(rstart * GRID_W, GRID_W)
        q2 = _half_masked_rows(q_ref[rr * GRID_W:(rr + 1) * GRID_W, :], 2)
        bias = jnp.concatenate([bias_ref[0, var, 0], bias_ref[0, var, 1]], axis=0)
        s_w = _nt(q2, k_ref[pl.ds(kstart, win), :]) + bias
        s_c = _nt(q2, kc_ref[...])
        o = _softmax_pv([s_w, s_c], [v_ref[pl.ds(kstart, win), :], vc_ref[...]])
        o_ref[rr * GRID_W:(rr + 1) * GRID_W, :] = _merge_halves(o, 2, GRID_W)[0].astype(BF16)


def _na_c_kernel(q_ref, kc_ref, vc_ref, prev_ref, o_ref):
    del prev_ref
    m = q_ref.shape[0]
    q2 = _half_masked_rows(q_ref[...], 2)
    o = _softmax_pv([_nt(q2, kc_ref[...])], [vc_ref[...]])
    o_ref[...] = _merge_halves(o, 2, m)[0].astype(BF16)


def _gqa_x_kernel(q_ref, k_ref, v_ref, kc_ref, vc_ref, o_ref):
    grp = GQA_Q_HEADS // GQA_KV_HEADS
    q = q_ref[...]
    q4 = jnp.concatenate([q[:, j * LANES:(j + 1) * LANES] for j in range(grp)], axis=0)
    o = _softmax_pv([_nt(q4, k_ref[...]), _nt(q4, kc_ref[...])], [v_ref[...], vc_ref[...]])
    for j in range(grp):
        o_ref[:, j * LANES:(j + 1) * LANES] = o[j * GQA_QBLK:(j + 1) * GQA_QBLK].astype(BF16)


def _na_bias_table(rpb):
    col = jnp.arange(GRID_W)
    cstart = jnp.clip(col - NA_WIN_COLS // 2, 0, GRID_W - NA_WIN_COLS)
    col_ok = (col[None, :] >= cstart[:, None]) & (col[None, :] < cstart[:, None] + NA_WIN_COLS)
    dcol = jnp.clip(col[None, :] - col[:, None] + NA_WIN_COLS - 1, 0, 2 * NA_WIN_COLS - 2)
    variants = []
    for var in range(NA_WIN_ROWS):
        drow = var + jnp.arange(NA_WIN_ROWS)
        rel = rpb[:, drow[None, :, None], dcol[:, None, :]].astype(F32)
        bias = jnp.where(col_ok[:, None, :], rel, NEG_INF).reshape(NA_HEADS, GRID_W, NA_WIN_ROWS * GRID_W)
        variants.append(bias)
    tab = jnp.stack(variants, axis=0)
    tab = tab.reshape(NA_WIN_ROWS, NA_HEADS // 2, 2, GRID_W, NA_WIN_ROWS * GRID_W)
    return tab.transpose(1, 0, 2, 3, 4)


def _gla_constants():
    c = GLA_CHUNK
    t = jnp.arange(c)[:, None]
    s = jnp.arange(c)[None, :]
    mats = []
    for direction in range(2):
        rows = [(s <= t) if direction == 0 else (s >= t), jnp.ones((c, c), bool)]
        for lvl in range(GLA_LEVELS):
            half = 1 << lvl
            blk_start = (t >> (lvl + 1)) << (lvl + 1)
            if direction == 0:
                rows.append(s <= blk_start + half - 1)
            else:
                rows.append(s >= blk_start + half)
        mats.append(jnp.concatenate(rows, axis=0))
    return jnp.stack(mats).astype(BF16)


def _hgrn_kernel(zq_ref, zi_ref, zf_ref, lbl_ref, cm_ref, o_ref, state_ref, *, layer):
    direction = pl.program_id(1)
    c = GLA_CHUNK
    dk = LANES

    @pl.when(pl.program_id(2) == 0)
    def _():
        state_ref[...] = jnp.zeros_like(state_ref)

    logits = lbl_ref[...]
    e = jnp.exp(logits - logits.max(axis=0, keepdims=True))
    lb = e[:layer + 1].sum(axis=0, keepdims=True) / e.sum(axis=0, keepdims=True)

    f = lb + (1.0 - lb) * _sigmoid(zf_ref[...])
    logf = jnp.log(f)
    hi = logf.astype(BF16)
    r1 = logf - hi.astype(F32)
    mid = r1.astype(BF16)
    lo = (r1 - mid.astype(F32)).astype(BF16)
    cm = cm_ref[0]
    sums = _nn(cm, hi) + _nn(cm, mid) + _nn(cm, lo)

    kk = 1.0 - f
    qq = _silu(zq_ref[...])
    vv = zi_ref[...]

    row = lax.broadcasted_iota(jnp.int32, (c, dk), 0)
    trow = lax.broadcasted_iota(jnp.int32, (c, c), 0)
    scol = lax.broadcasted_iota(jnp.int32, (c, c), 1)
    q_half = 1 - direction

    for h in range(HGRN_HEADS):
        sl = slice(h * dk, (h + 1) * dk)
        b = sums[0:c, sl]
        tot = sums[c:2 * c, sl]
        q, k, v = qq[:, sl], kk[:, sl], vv[:, sl]
        vb = v.astype(BF16)
        att = jnp.zeros((c, c), F32)
        for lvl in range(GLA_LEVELS):
            ref = sums[(2 + lvl) * c:(3 + lvl) * c, sl]
            is_q = ((row >> lvl) & 1) == q_half
            qh = jnp.where(is_q, jnp.exp(jnp.where(is_q, b - ref, 0.0)), 0.0) * q
            kh = jnp.where(is_q, 0.0, jnp.exp(jnp.where(is_q, 0.0, ref - b))) * k
            same = (trow >> (lvl + 1)) == (scol >> (lvl + 1))
            att = att + jnp.where(same, _nt(qh.astype(BF16), kh.astype(BF16)), 0.0)
        o = _nn(att.astype(BF16), vb) + jnp.sum(q * k, axis=-1, keepdims=True) * v
        state = state_ref[h]
        o = o + _nn((q * jnp.exp(b)).astype(BF16), state.astype(BF16))
        kt = (k * jnp.exp(tot - b)).T.astype(BF16)
        state_ref[h] = jnp.exp(tot[0:1, :]).T * state + _nn(kt, vb)
        o_ref[0, :, sl] = o


def _hgrn_core(z, lb_logits, layer, n_batch, seq, n_ctx):
    t_all, n5 = z.shape
    d = n5 // 5
    c = GLA_CHUNK
    n_chunks_x, n_chunks_c = seq // c, n_ctx // c
    n_chunks = n_chunks_x + n_chunks_c
    ctx_base = n_batch * n_chunks_x

    def row_block(b, direction, ci):
        fwd = jnp.where(ci < n_chunks_c, ctx_base + b * n_chunks_c + ci, b * n_chunks_x + ci - n_chunks_c)
        cj = n_chunks - 1 - ci
        bwd = jnp.where(cj < n_chunks_x, b * n_chunks_x + cj, ctx_base + b * n_chunks_c + cj - n_chunks_x)
        return jnp.where(direction == 0, fwd, bwd)

    n_lb = lb_logits.shape[0]
    return pl.pallas_call(
        functools.partial(_hgrn_kernel, layer=layer),
        grid=(n_batch, 2, n_chunks),
        in_specs=[pl.BlockSpec((c, d), lambda b, dr, ci: (row_block(b, dr, ci), 0)),
                  pl.BlockSpec((c, d), lambda b, dr, ci: (row_block(b, dr, ci), 1)),
                  pl.BlockSpec((c, d), lambda b, dr, ci: (row_block(b, dr, ci), 2 + dr)),
                  pl.BlockSpec((n_lb, d), lambda b, dr, ci: (0, 0)),
                  pl.BlockSpec((1, 8 * c, c), lambda b, dr, ci: (dr, 0, 0))],
        out_specs=pl.BlockSpec((1, c, d), lambda b, dr, ci: (dr, row_block(b, dr, ci), 0)),
        out_shape=jax.ShapeDtypeStruct((2, t_all, d), F32),
        scratch_shapes=[pltpu.VMEM((HGRN_HEADS, LANES, LANES), F32)],
        compiler_params=_cparams("parallel", "parallel", "arbitrary"),
        name="hgrn_gla",
    )(z, z, z, lb_logits, _gla_constants())


def _out_attn_kernel(a_ref, w_ref, x_ref, mod_ref, g_ref, b_ref, o_ref, *, alpha):
    y = _nn(a_ref[...], w_ref[...])
    u = alpha * x_ref[...] + mod_ref[0, 2:3, :] * y
    o_ref[...] = _layer_norm(u, g_ref[...], b_ref[...])


def _out_hgrn_kernel(of_ref, ob_ref, zg_ref, ng_ref, w_ref, x_ref, mod_ref, g_ref, b_ref, o_ref, *, alpha):
    o = of_ref[0] + ob_ref[0]
    gate = _silu(zg_ref[...])
    parts = []
    for h in range(HGRN_HEADS):
        sl = slice(h * LANES, (h + 1) * LANES)
        parts.append((_rms_block(o[:, sl], ng_ref[...]) * gate[:, sl]).astype(BF16))
    y = _nn(jnp.concatenate(parts, axis=1), w_ref[...])
    u = alpha * x_ref[...] + mod_ref[0, 2:3, :] * y
    o_ref[...] = _layer_norm(u, g_ref[...], b_ref[...])


def _top16_rows(s, n_rows):
    iota = lax.broadcasted_iota(jnp.int32, s.shape, 0).astype(F32)
    rank = jnp.full(s.shape, float(PEER_TOPK), F32)
    vals = []
    for a in range(PEER_TOPK):
        m = s.max(axis=0, keepdims=True)
        first = jnp.where(s == m, iota, float(n_rows)).min(axis=0, keepdims=True)
        sel = iota == first
        rank = jnp.where(sel, float(a), rank)
        s = jnp.where(sel, -jnp.inf, s)
        vals.append(m)
    return vals, rank


def _peer_select_kernel(x_ref, mod_ref, wq_ref, keys_ref, a_ref, n_ref, b_ref, r_ref):
    tok = _modulate(x_ref, mod_ref, 3).astype(BF16)
    qt = _nt(wq_ref[...], tok).astype(BF16)
    tt = tok.shape[0]
    nk = PEER_NKEYS
    i16 = lax.broadcasted_iota(jnp.int32, (PEER_TOPK, tt), 0)
    for h in range(PEER_HEADS):
        s, vals, rank = [], [], []
        for p in range(2):
            hp = 2 * h + p
            sp = _nn(keys_ref[hp], qt[hp * nk:(hp + 1) * nk, :])
            vp, rp = _top16_rows(sp, nk)
            s.append(sp), vals.append(vp), rank.append(rp)
        sv1 = jnp.zeros((PEER_TOPK, tt), F32)
        for bq in range(PEER_TOPK):
            sv1 = jnp.where(i16 == bq, vals[1][bq], sv1)
        cand = jnp.concatenate([vals[0][a] + sv1 for a in range(PEER_TOPK)], axis=0)
        _, crank = _top16_rows(cand, PEER_TOPK * PEER_TOPK)
        chosen = crank < float(PEER_TOPK)
        cmax = vals[0][0] + vals[1][0]
        zsum = jnp.where(chosen, jnp.exp(cand - cmax), 0.0).sum(axis=0, keepdims=True)
        nb = jnp.zeros((nk, tt), F32)
        for a in range(PEER_TOPK):
            n_a = chosen[a * PEER_TOPK:(a + 1) * PEER_TOPK].astype(F32).sum(axis=0, keepdims=True)
            nb = jnp.where(rank[0] == float(a), n_a, nb)
        a_ref[h] = jnp.where(rank[0] < float(PEER_TOPK), jnp.exp(s[0] - vals[0][0]), 0.0) / zsum
        n_ref[h] = nb
        b_ref[h] = jnp.where(rank[1] < float(PEER_TOPK), jnp.exp(s[1] - vals[1][0]), 0.0)
        r_ref[h] = rank[1]


def _gelu_tanh(x):
    return 0.5 * x * (1.0 + jnp.tanh(0.7978845608028654 * (x + 0.044715 * (x * x * x))))


def _peer_dense_kernel(x_ref, mod_ref, a_ref, n_ref, b_ref, r_ref, u_ref, vt_ref, g_ref, beta_ref, o_ref,
                       tok_ref, acc_ref, p_ref, *, alpha):
    e = pl.program_id(1)
    nk = PEER_NKEYS

    @pl.when(e == 0)
    def _():
        tok_ref[...] = _modulate(x_ref, mod_ref, 3).astype(BF16)
        acc_ref[...] = jnp.zeros_like(acc_ref)

    act = _nt(u_ref[...], tok_ref[...])
    i0 = pl.multiple_of(e * PEER_IBLK, PEER_IBLK)
    for ii in range(PEER_IBLK):
        w = None
        for h in range(PEER_HEADS):
            a_row = a_ref[h, pl.ds(i0, PEER_IBLK), :][ii:ii + 1, :]
            n_row = n_ref[h, pl.ds(i0, PEER_IBLK), :][ii:ii + 1, :]
            term = a_row * jnp.where(r_ref[h] < n_row, b_ref[h], 0.0)
            w = term if w is None else w + term
        p_ref[ii * nk:(ii + 1) * nk, :] = (w * _gelu_tanh(act[ii * nk:(ii + 1) * nk, :])).astype(BF16)
    acc_ref[...] += _nn(vt_ref[...], p_ref[...])

    @pl.when(e == pl.num_programs(1) - 1)
    def _():
        y = acc_ref[...].T
        u = alpha * x_ref[...] + mod_ref[0, 5:6, :] * y
        o_ref[...] = _layer_norm(u, g_ref[...], beta_ref[...])


def _peer(x1, mod_l, wq_t, keys, u_bf, vt_bf, ln_g, ln_b, alpha, seq, n_batch, n_rows):
    d = x1.shape[1]
    tt = PEER_TOK_TILE
    nk = PEER_NKEYS
    n_tiles = n_rows // tt
    mod_map = _mod_row_map(tt, seq, n_batch)
    sel_shape = jax.ShapeDtypeStruct((PEER_HEADS, nk, n_rows), F32)
    sel_spec = pl.BlockSpec((PEER_HEADS, nk, tt), lambda i: (0, 0, i))
    a, n, b, r = pl.pallas_call(
        _peer_select_kernel,
        grid=(n_tiles,),
        in_specs=[pl.BlockSpec((tt, d), lambda i: (i, 0)),
                  pl.BlockSpec((1, 6, d), mod_map),
                  pl.BlockSpec(wq_t.shape, lambda i: (0, 0)),
                  pl.BlockSpec(keys.shape, lambda i: (0, 0, 0))],
        out_specs=[sel_spec] * 4,
        out_shape=[sel_shape] * 4,
        compiler_params=_cparams("parallel"),
        name="peer_select",
    )(x1, mod_l, wq_t, keys)

    et = PEER_IBLK * nk
    n_eblk = u_bf.shape[0] // et
    sel_spec2 = pl.BlockSpec((PEER_HEADS, nk, tt), lambda i, e: (0, 0, i))
    return pl.pallas_call(
        functools.partial(_peer_dense_kernel, alpha=alpha),
        grid=(n_tiles, n_eblk),
        in_specs=[pl.BlockSpec((tt, d), lambda i, e: (i, 0)),
                  pl.BlockSpec((1, 6, d), lambda i, e: mod_map(i)),
                  sel_spec2, sel_spec2, sel_spec2, sel_spec2,
                  pl.BlockSpec((et, d), lambda i, e: (e, 0)),
                  pl.BlockSpec((d, et), lambda i, e: (0, e)),
                  pl.BlockSpec((1, d), lambda i, e: (0, 0)),
                  pl.BlockSpec((1, d), lambda i, e: (0, 0))],
        out_specs=pl.BlockSpec((tt, d), lambda i, e: (i, 0)),
        out_shape=jax.ShapeDtypeStruct((n_rows, d), F32),
        scratch_shapes=[pltpu.VMEM((tt, d), BF16), pltpu.VMEM((d, tt), F32), pltpu.VMEM((et, tt), BF16)],
        compiler_params=_cparams("parallel", "arbitrary"),
        name="peer_dense",
    )(x1, mod_l, a, n, b, r, u_bf, vt_bf, ln_g, ln_b)


def _attention_outputs(kind, q, k, v, params, n_batch, seq, n_ctx, with_ctx):
    t_lat = n_batch * seq
    t_all, d = q.shape
    ctx_blk = t_lat // n_ctx
    out_rows = t_all if with_ctx else t_lat
    out_shape = jax.ShapeDtypeStruct((out_rows, d), BF16)
    smem = pl.BlockSpec(memory_space=pltpu.SMEM)
    if kind == 1:
        sink = params
        n_q = seq // SWA_QBLK
        qw = 2 * LANES
        out = pl.pallas_call(
            functools.partial(_swa_x_kernel, seq=seq),
            grid=(n_batch, SWA_KV_HEADS, n_q),
            in_specs=[smem,
                      pl.BlockSpec((SWA_QBLK, qw), lambda b, g, n: (b * n_q + n, g)),
                      pl.BlockSpec((seq, LANES), lambda b, g, n: (b, g)),
                      pl.BlockSpec((seq, LANES), lambda b, g, n: (b, g)),
                      pl.BlockSpec((n_ctx, LANES), lambda b, g, n: (ctx_blk + b, g)),
                      pl.BlockSpec((n_ctx, LANES), lambda b, g, n: (ctx_blk + b, g))],
            out_specs=pl.BlockSpec((SWA_QBLK, qw), lambda b, g, n: (b * n_q + n, g)),
            out_shape=out_shape,
            compiler_params=_cparams("parallel", "parallel", "arbitrary"),
            name="swa_latent",
        )(sink, q, k, v, k, v)
        if with_ctx:
            out = pl.pallas_call(
                _swa_c_kernel,
                grid=(n_batch, SWA_KV_HEADS),
                in_specs=[smem,
                          pl.BlockSpec((n_ctx, qw), lambda b, g: (ctx_blk + b, g)),
                          pl.BlockSpec((n_ctx, LANES), lambda b, g: (ctx_blk + b, g)),
                          pl.BlockSpec((n_ctx, LANES), lambda b, g: (ctx_blk + b, g)),
                          pl.BlockSpec(memory_space=pl.ANY)],
                out_specs=pl.BlockSpec((n_ctx, qw), lambda b, g: (ctx_blk + b, g)),
                out_shape=out_shape,
                input_output_aliases={4: 0},
                compiler_params=_cparams("parallel", "parallel"),
                name="swa_context",
            )(sink, q, k, v, out)
        return out
    if kind == 2:
        bias = params
        n_pairs = NA_HEADS // 2
        n_grid_rows = seq // GRID_W
        rows_step = NA_ROWS_PER_STEP * GRID_W
        n_q = seq // rows_step
        out = pl.pallas_call(
            functools.partial(_na_x_kernel, n_grid_rows=n_grid_rows),
            grid=(n_batch, n_pairs, n_q),
            in_specs=[pl.BlockSpec((rows_step, LANES), lambda b, p, n: (b * n_q + n, p)),
                      pl.BlockSpec((seq, LANES), lambda b, p, n: (b, p)),
                      pl.BlockSpec((seq, LANES), lambda b, p, n: (b, p)),
                      pl.BlockSpec((n_ctx, LANES), lambda b, p, n: (ctx_blk + b, p)),
                      pl.BlockSpec((n_ctx, LANES), lambda b, p, n: (ctx_blk + b, p)),
                      pl.BlockSpec((1,) + bias.shape[1:], lambda b, p, n: (p, 0, 0, 0, 0))],
            out_specs=pl.BlockSpec((rows_step, LANES), lambda b, p, n: (b * n_q + n, p)),
            out_shape=out_shape,
            compiler_params=_cparams("parallel", "parallel", "arbitrary"),
            name="na_latent",
        )(q, k, v, k, v, bias)
        if with_ctx:
            out = pl.pallas_call(
                _na_c_kernel,
                grid=(n_batch, n_pairs),
                in_specs=[pl.BlockSpec((n_ctx, LANES), lambda b, p: (ctx_blk + b, p)),
                          pl.BlockSpec((n_ctx, LANES), lambda b, p: (ctx_blk + b, p)),
                          pl.BlockSpec((n_ctx, LANES), lambda b, p: (ctx_blk + b, p)),
                          pl.BlockSpec(memory_space=pl.ANY)],
                out_specs=pl.BlockSpec((n_ctx, LANES), lambda b, p: (ctx_blk + b, p)),
                out_shape=out_shape,
                input_output_aliases={3: 0},
                compiler_params=_cparams("parallel", "parallel"),
                name="na_context",
            )(q, k, v, out)
        return out
    assert kind == 3 and not with_ctx
    n_q = seq // GQA_QBLK
    qw = (GQA_Q_HEADS // GQA_KV_HEADS) * LANES
    return pl.pallas_call(
        _gqa_x_kernel,
        grid=(n_batch, GQA_KV_HEADS, n_q),
        in_specs=[pl.BlockSpec((GQA_QBLK, qw), lambda b, g, n: (b * n_q + n, g)),
                  pl.BlockSpec((seq, LANES), lambda b, g, n: (b, g)),
                  pl.BlockSpec((seq, LANES), lambda b, g, n: (b, g)),
                  pl.BlockSpec((n_ctx, LANES), lambda b, g, n: (ctx_blk + b, g)),
                  pl.BlockSpec((n_ctx, LANES), lambda b, g, n: (ctx_blk + b, g))],
        out_specs=pl.BlockSpec((GQA_QBLK, qw), lambda b, g, n: (b * n_q + n, g)),
        out_shape=out_shape,
        compiler_params=_cparams("parallel", "parallel", "arbitrary"),
        name="gqa_latent",
    )(q, k, v, k, v)


def _out_proj(kernel, lead_args, lead_specs, w_out, x, mod_l, ln_g, ln_b, tm, seq, n_batch, n_rows):
    d = x.shape[1]
    row = lambda i: (i, 0)
    const = lambda i: (0, 0)
    return pl.pallas_call(
        kernel,
        grid=(n_rows // tm,),
        in_specs=lead_specs + [pl.BlockSpec((d, d), const),
                               pl.BlockSpec((tm, d), row),
                               pl.BlockSpec((1, 6, d), _mod_row_map(tm, seq, n_batch)),
                               pl.BlockSpec((1, d), const),
                               pl.BlockSpec((1, d), const)],
        out_specs=pl.BlockSpec((tm, d), row),
        out_shape=jax.ShapeDtypeStruct((n_rows, d), F32),
        compiler_params=_cparams("parallel"),
        name="out_proj_ln",
    )(*lead_args, w_out, x, mod_l, ln_g, ln_b)


def kernel(x, c, ctx, c_ctx, mod_w, mod_b, ln_g, ln_b, peer_wq, peer_keys, peer_u, peer_v, hgrn_w_in, hgrn_lb_logits, hgrn_norm_g, hgrn_w_out, swa_w_in, swa_sink, swa_w_out, na_w_in, na_rpb, na_w_out, gqa_w_in, gqa_q_norm, gqa_k_norm, gqa_w_out):
    n_batch, seq, d = x.shape
    n_ctx = ctx.shape[1]
    depth = mod_w.shape[0]
    alpha = (2 * depth) ** 0.25
    t_lat = n_batch * seq
    t_ctx = n_batch * n_ctx
    assert n_batch < 16 and seq % (NA_WIN_ROWS * GRID_W) == 0 and seq % n_ctx == 0
    assert (t_lat % PEER_TOK_TILE == 0) and (t_ctx % PEER_TOK_TILE == 0) and seq % PEER_TOK_TILE == 0

    xs = jnp.concatenate([x.reshape(t_lat, d), ctx.reshape(t_ctx, d)], axis=0)
    cond = jnp.zeros((16, d), F32).at[:n_batch].set(c).at[n_batch].set(c_ctx)
    mod = _modulation(cond, mod_w, mod_b).reshape(depth, 16, 6, d)

    tm = _row_tile(seq, t_ctx, 512)
    row = lambda i: (i, 0)

    for i in range(depth):
        kind, occ = i % 4, i // 4
        last = i == depth - 1
        n_rows = t_lat if last else t_lat + t_ctx
        mod_l = mod[i]
        g1, b1 = ln_g[i, 0][None], ln_b[i, 0][None]
        g2, b2 = ln_g[i, 1][None], ln_b[i, 1][None]
        if kind == 0:
            tmh = _row_tile(seq, t_ctx, 256)
            z = _in_proj(_in_hgrn_kernel, xs, mod_l, hgrn_w_in[occ].astype(BF16), [], [], [5 * d], F32,
                         tmh, seq, n_batch)
            o2 = _hgrn_core(z, hgrn_lb_logits, i, n_batch, seq, n_ctx)
            lead_specs = [pl.BlockSpec((1, tmh, d), lambda j: (0, j, 0)),
                          pl.BlockSpec((1, tmh, d), lambda j: (1, j, 0)),
                          pl.BlockSpec((tmh, d), lambda j: (j, 4)),
                          pl.BlockSpec((1, LANES), lambda j: (0, 0))]
            x1 = _out_proj(functools.partial(_out_hgrn_kernel, alpha=alpha),
                           [o2, o2, z, hgrn_norm_g[occ][None]], lead_specs, hgrn_w_out[occ].astype(BF16),
                           xs, mod_l, g1, b1, tmh, seq, n_batch, n_rows)
        else:
            if kind == 1:
                cos_t, sin_t = _rope_tables(seq, 64, tm)
                q, k, v = _in_proj(_in_swa_kernel, xs, mod_l, swa_w_in[occ].astype(BF16), [cos_t, sin_t],
                                   _rope_specs(tm, seq, t_lat), [d, 2 * SWA_KV_HEADS * 64, 2 * SWA_KV_HEADS * 64], BF16,
                                   tm, seq, n_batch)
                params, w_out = swa_sink[occ], swa_w_out[occ]
            elif kind == 2:
                q, k, v = _in_proj(_in_na_kernel, xs, mod_l, na_w_in[occ].astype(BF16), [], [], [d, d, d], BF16,
                                   tm, seq, n_batch)
                params, w_out = _na_bias_table(na_rpb[occ]), na_w_out[occ]
            else:
                cos_t, sin_t = _rope_tables(seq, LANES, tm)
                norm_spec = pl.BlockSpec((1, LANES), lambda j: (0, 0))
                q, k, v = _in_proj(_in_gqa_kernel, xs, mod_l, gqa_w_in[occ].astype(BF16),
                                   [cos_t, sin_t, gqa_q_norm[occ][None], gqa_k_norm[occ][None]],
                                   _rope_specs(tm, seq, t_lat) + [norm_spec, norm_spec],
                                   [d, GQA_KV_HEADS * LANES, GQA_KV_HEADS * LANES], BF16, tm, seq, n_batch)
                params, w_out = None, gqa_w_out[occ]
            att = _attention_outputs(kind, q, k, v, params, n_batch, seq, n_ctx, not last)
            x1 = _out_proj(functools.partial(_out_attn_kernel, alpha=alpha), [att], [pl.BlockSpec((tm, d), row)],
                           w_out.astype(BF16), xs, mod_l, g1, b1, tm, seq, n_batch, n_rows)
        keys = peer_keys[i].reshape(2 * PEER_HEADS, PEER_NKEYS, -1).astype(BF16)
        xs = _peer(x1, mod_l, peer_wq[i].T.astype(BF16), keys, peer_u[i].astype(BF16),
                   peer_v[i].T.astype(BF16), g2, b2, alpha, seq, n_batch, n_rows)
    return xs[:t_lat].reshape(n_batch, seq, d)
```

```python
import functools

import jax
import jax.numpy as jnp
from jax import lax
from jax.experimental import pallas as pl
from jax.experimental.pallas import tpu as pltpu

F32 = jnp.float32
BF16 = jnp.bfloat16

LANES = 128
GRID_W = 64
LN_EPS = 1e-5
RMS_EPS = 1e-6
NEG_INF = -1e30
ROPE_THETA = 10000.0

HGRN_HEADS = 8
GLA_CHUNK = 64
GLA_LEVELS = 6

SWA_Q_HEADS = 16
SWA_KV_HEADS = 4
SWA_WINDOW = 128
SWA_QBLK = 128

NA_HEADS = 16
NA_WIN_ROWS = 8
NA_WIN_COLS = 16
NA_ROWS_PER_STEP = 4

GQA_Q_HEADS = 8
GQA_KV_HEADS = 2
GQA_QBLK = 128

PEER_HEADS = 8
PEER_NKEYS = 128
PEER_TOPK = 16
PEER_TOK_TILE = 256
PEER_IBLK = 8
PEER_JCHUNK = 32

VMEM_LIMIT = 56 * 1024 * 1024


def _cparams(*sem):
    return pltpu.CompilerParams(dimension_semantics=sem, vmem_limit_bytes=VMEM_LIMIT)


def _nt(a, b):
    return lax.dot_general(a, b, (((1,), (1,)), ((), ())), preferred_element_type=F32)


def _nn(a, b):
    return jnp.dot(a, b, preferred_element_type=F32)


def _sigmoid(x):
    return 1.0 / (1.0 + jnp.exp(-x))


def _silu(x):
    return x * _sigmoid(x)


def _layer_norm(u, g, b):
    mu = jnp.mean(u, axis=-1, keepdims=True)
    d = u - mu
    var = jnp.mean(d * d, axis=-1, keepdims=True)
    return d * lax.rsqrt(var + LN_EPS) * g + b


def _row_tile(seq, n_ctx_rows, pref):
    for t in (512, 256, 128, 64):
        if t <= pref and seq % t == 0 and n_ctx_rows % t == 0:
            return t
    raise ValueError("no row tile")


def _mod_row_map(tm, seq, n_batch):
    return lambda i: (jnp.minimum((i * tm) // seq, n_batch), 0, 0)


def _mod_kernel(c_ref, w_ref, b_ref, o_ref):
    h = _silu(c_ref[...]).astype(BF16)
    o_ref[0] = _nn(h, w_ref[0].astype(BF16)) + b_ref[0]


def _modulation(cond, mod_w, mod_b):
    depth, d, n = mod_w.shape
    tn = 1536
    return pl.pallas_call(
        _mod_kernel,
        grid=(depth, n // tn),
        in_specs=[pl.BlockSpec((16, d), lambda l, j: (0, 0)),
                  pl.BlockSpec((1, d, tn), lambda l, j: (l, 0, j)),
                  pl.BlockSpec((1, 1, tn), lambda l, j: (l, 0, j))],
        out_specs=pl.BlockSpec((1, 16, tn), lambda l, j: (l, 0, j)),
        out_shape=jax.ShapeDtypeStruct((depth, 16, n), F32),
        compiler_params=_cparams("parallel", "parallel"),
        name="modulation",
    )(cond, mod_w, mod_b.reshape(depth, 1, n))


def _rope_tables(seq, dh, tm):
    t = jnp.arange(seq)
    pos = jnp.stack([t // GRID_W, t % GRID_W]).astype(F32)
    d_axis = dh // 2
    inv = ROPE_THETA ** (-jnp.arange(0, d_axis, 2, dtype=F32) / d_axis)
    ang = pos[:, :, None] * inv
    cos, sin = jnp.cos(ang), jnp.sin(ang)
    cos_h = jnp.concatenate([cos[0], cos[0], cos[1], cos[1]], axis=-1)
    sin_h = jnp.concatenate([-sin[0], sin[0], -sin[1], sin[1]], axis=-1)
    reps = LANES // dh
    cos_t = jnp.concatenate([jnp.tile(cos_h, (1, reps)), jnp.ones((tm, LANES), F32)], axis=0)
    sin_t = jnp.concatenate([jnp.tile(sin_h, (1, reps)), jnp.zeros((tm, LANES), F32)], axis=0)
    return cos_t, sin_t


def _rope_block(zb, cos, sin, quarter):
    lane = lax.broadcasted_iota(jnp.int32, zb.shape, 1)
    first = (lane % (2 * quarter)) < quarter
    up = pltpu.roll(zb, LANES - quarter, 1)
    down = pltpu.roll(zb, quarter, 1)
    return zb * cos + jnp.where(first, up, down) * sin


def _dup_half(blk, half):
    lane = lax.broadcasted_iota(jnp.int32, blk.shape, 1)
    swapped = pltpu.roll(blk, 64, 1)
    keep = (lane < 64) if half == 0 else (lane >= 64)
    return jnp.where(keep, blk, swapped)


def _modulate(x_ref, mod_ref, shift_row):
    return x_ref[...] * (1.0 + mod_ref[0, shift_row + 1:shift_row + 2, :]) + mod_ref[0, shift_row:shift_row + 1, :]


def _in_hgrn_kernel(x_ref, mod_ref, w_ref, o_ref):
    h = _modulate(x_ref, mod_ref, 0).astype(BF16)
    o_ref[...] = _nn(h, w_ref[...])


def _in_swa_kernel(x_ref, mod_ref, w_ref, cos_ref, sin_ref, q_ref, k_ref, v_ref):
    h = _modulate(x_ref, mod_ref, 0).astype(BF16)
    z = _nn(h, w_ref[...])
    cos, sin = cos_ref[...], sin_ref[...]
    dq = SWA_Q_HEADS * 64
    dkv = SWA_KV_HEADS * 64
    for blk in range(dq // LANES):
        zb = z[:, blk * LANES:(blk + 1) * LANES]
        q_ref[:, blk * LANES:(blk + 1) * LANES] = (_rope_block(zb, cos, sin, 16) * 0.125).astype(BF16)
    for blk in range(dkv // LANES):
        kb = _rope_block(z[:, dq + blk * LANES:dq + (blk + 1) * LANES], cos, sin, 16)
        vb = z[:, dq + dkv + blk * LANES:dq + dkv + (blk + 1) * LANES]
        for half in range(2):
            kv = 2 * blk + half
            k_ref[:, kv * LANES:(kv + 1) * LANES] = _dup_half(kb, half).astype(BF16)
            v_ref[:, kv * LANES:(kv + 1) * LANES] = _dup_half(vb, half).astype(BF16)


def _in_na_kernel(x_ref, mod_ref, w_ref, q_ref, k_ref, v_ref):
    h = _modulate(x_ref, mod_ref, 0).astype(BF16)
    z = _nn(h, w_ref[...])
    d = q_ref.shape[1]
    q_ref[...] = (z[:, :d] * 0.125).astype(BF16)
    k_ref[...] = z[:, d:2 * d].astype(BF16)
    v_ref[...] = z[:, 2 * d:].astype(BF16)


def _rms_block(zb, g):
    return zb * lax.rsqrt(jnp.mean(zb * zb, axis=-1, keepdims=True) + RMS_EPS) * g


def _in_gqa_kernel(x_ref, mod_ref, w_ref, cos_ref, sin_ref, qn_ref, kn_ref, q_ref, k_ref, v_ref):
    h = _modulate(x_ref, mod_ref, 0).astype(BF16)
    z = _nn(h, w_ref[...])
    cos, sin = cos_ref[...], sin_ref[...]
    dq = GQA_Q_HEADS * LANES
    dkv = GQA_KV_HEADS * LANES
    scale = LANES ** -0.5
    for hd in range(GQA_Q_HEADS):
        zb = _rms_block(z[:, hd * LANES:(hd + 1) * LANES], qn_ref[...])
        q_ref[:, hd * LANES:(hd + 1) * LANES] = (_rope_block(zb, cos, sin, 32) * scale).astype(BF16)
    for hd in range(GQA_KV_HEADS):
        zb = _rms_block(z[:, dq + hd * LANES:dq + (hd + 1) * LANES], kn_ref[...])
        k_ref[:, hd * LANES:(hd + 1) * LANES] = _rope_block(zb, cos, sin, 32).astype(BF16)
    v_ref[...] = z[:, dq + dkv:].astype(BF16)


def _in_proj(kernel, x, mod_l, w, extra, extra_specs, out_widths, out_dtype, tm, seq, n_batch, n_rows=None):
    t_all, d = x.shape
    n_rows = t_all if n_rows is None else n_rows
    n = w.shape[1]
    in_specs = [pl.BlockSpec((tm, d), lambda i: (i, 0)),
                pl.BlockSpec((1, 6, d), _mod_row_map(tm, seq, n_batch)),
                pl.BlockSpec((d, n), lambda i: (0, 0))] + extra_specs
    outs = [jax.ShapeDtypeStruct((n_rows, wd), out_dtype) for wd in out_widths]
    out_specs = [pl.BlockSpec((tm, wd), lambda i: (i, 0)) for wd in out_widths]
    single = len(outs) == 1
    res = pl.pallas_call(
        kernel,
        grid=(n_rows // tm,),
        in_specs=in_specs,
        out_specs=out_specs[0] if single else out_specs,
        out_shape=outs[0] if single else outs,
        compiler_params=_cparams("parallel"),
        name=kernel.__name__.strip("_"),
    )(x, mod_l, w, *extra)
    return res


def _rope_specs(tm, seq, n_lat):
    idx = lambda i: (jnp.where(i * tm < n_lat, (i * tm % seq) // tm, seq // tm), 0)
    return [pl.BlockSpec((tm, LANES), idx), pl.BlockSpec((tm, LANES), idx)]


def _softmax_pv(scores, values, extra_logit=None):
    m = scores[0].max(axis=-1, keepdims=True)
    for s in scores[1:]:
        m = jnp.maximum(m, s.max(axis=-1, keepdims=True))
    if extra_logit is not None:
        m = jnp.maximum(m, extra_logit)
    den = jnp.exp(extra_logit - m) if extra_logit is not None else 0.0
    acc = None
    for s, v in zip(scores, values):
        p = jnp.exp(s - m)
        den = den + p.sum(axis=-1, keepdims=True)
        o = _nn(p.astype(BF16), v)
        acc = o if acc is None else acc + o
    return acc / den


def _half_masked_rows(q, n_heads):
    m = q.shape[0]
    lane = lax.broadcasted_iota(jnp.int32, (m, LANES), 1)
    rows = []
    for j in range(n_heads):
        blk = q[:, (j // 2) * LANES:(j // 2 + 1) * LANES]
        keep = (lane < 64) if j % 2 == 0 else (lane >= 64)
        rows.append(jnp.where(keep, blk, jnp.zeros_like(blk)))
    return jnp.concatenate(rows, axis=0)


def _merge_halves(o, n_heads, m):
    lane = lax.broadcasted_iota(jnp.int32, (m, LANES), 1)
    outs = []
    for b in range(n_heads // 2):
        outs.append(jnp.where(lane < 64, o[(2 * b) * m:(2 * b + 1) * m], o[(2 * b + 1) * m:(2 * b + 2) * m]))
    return outs


def _sink_column(sink_ref, first_head, n_heads, m):
    return jnp.concatenate([jnp.full((m, 1), sink_ref[first_head + j], F32) for j in range(n_heads)], axis=0)


def _swa_x_kernel(sink_ref, q_ref, k_ref, v_ref, kc_ref, vc_ref, o_ref, *, seq):
    g, n = pl.program_id(1), pl.program_id(2)
    grp = SWA_Q_HEADS // SWA_KV_HEADS
    span = SWA_QBLK + 2 * SWA_WINDOW
    start = n * SWA_QBLK
    cstart = pl.multiple_of(jnp.clip(start - SWA_WINDOW, 0, seq - span), SWA_QBLK)
    q4 = _half_masked_rows(q_ref[...], grp)
    s_b = _nt(q4, k_ref[pl.ds(cstart, span), :])
    s_c = _nt(q4, kc_ref[...])
    qpos = start + lax.broadcasted_iota(jnp.int32, s_b.shape, 0) % SWA_QBLK
    kpos = cstart + lax.broadcasted_iota(jnp.int32, s_b.shape, 1)
    s_b = jnp.where(jnp.abs(qpos - kpos) <= SWA_WINDOW, s_b, NEG_INF)
    sk = _sink_column(sink_ref, g * grp, grp, SWA_QBLK)
    o = _softmax_pv([s_b, s_c], [v_ref[pl.ds(cstart, span), :], vc_ref[...]], sk)
    for b, ob in enumerate(_merge_halves(o, grp, SWA_QBLK)):
        o_ref[:, b * LANES:(b + 1) * LANES] = ob.astype(BF16)


def _swa_c_kernel(sink_ref, q_ref, kc_ref, vc_ref, prev_ref, o_ref):
    del prev_ref
    g = pl.program_id(1)
    grp = SWA_Q_HEADS // SWA_KV_HEADS
    m = q_ref.shape[0]
    q4 = _half_masked_rows(q_ref[...], grp)
    sk = _sink_column(sink_ref, g * grp, grp, m)
    o = _softmax_pv([_nt(q4, kc_ref[...])], [vc_ref[...]], sk)
    for b, ob in enumerate(_merge_halves(o, grp, m)):
        o_ref[:, b * LANES:(b + 1) * LANES] = ob.astype(BF16)


def _na_x_kernel(q_ref, k_ref, v_ref, kc_ref, vc_ref, bias_ref, o_ref, *, n_grid_rows):
    rblk = pl.program_id(2)
    win = NA_WIN_ROWS * GRID_W
    for rr in range(NA_ROWS_PER_STEP):
        r = rblk * NA_ROWS_PER_STEP + rr
        rstart = jnp.clip(r - NA_WIN_ROWS // 2, 0, n_grid_rows - NA_WIN_ROWS)
        var = rstart - r + NA_WIN_ROWS - 1
        kstart = pl.multiple_of(rstart * GRID_W, GRID_W)
        q2 = _half_masked_rows(q_ref[rr * GRID_W:(rr + 1) * GRID_W, :], 2)
        bias = jnp.concatenate([bias_ref[0, var, 0], bias_ref[0, var, 1]], axis=0)
        s_w = _nt(q2, k_ref[pl.ds(kstart, win), :]) + bias
        s_c = _nt(q2, kc_ref[...])
        o = _softmax_pv([s_w, s_c], [v_ref[pl.ds(kstart, win), :], vc_ref[...]])
        o_ref[rr * GRID_W:(rr + 1) * GRID_W, :] = _merge_halves(o, 2, GRID_W)[0].astype(BF16)


def _na_c_kernel(q_ref, kc_ref, vc_ref, prev_ref, o_ref):
    del prev_ref
    m = q_ref.shape[0]
    q2 = _half_masked_rows(q_ref[...], 2)
    o = _softmax_pv([_nt(q2, kc_ref[...])], [vc_ref[...]])
    o_ref[...] = _merge_halves(o, 2, m)[0].astype(BF16)


def _gqa_x_kernel(q_ref, k_ref, v_ref, kc_ref, vc_ref, o_ref):
    grp = GQA_Q_HEADS // GQA_KV_HEADS
    q = q_ref[...]
    q4 = jnp.concatenate([q[:, j * LANES:(j + 1) * LANES] for j in range(grp)], axis=0)
    o = _softmax_pv([_nt(q4, k_ref[...]), _nt(q4, kc_ref[...])], [v_ref[...], vc_ref[...]])
    for j in range(grp):
        o_ref[:, j * LANES:(j + 1) * LANES] = o[j * GQA_QBLK:(j + 1) * GQA_QBLK].astype(BF16)


def _na_bias_kernel(rpb_ref, o_ref):
    n_dr, n_dc = 2 * NA_WIN_ROWS - 1, 2 * NA_WIN_COLS - 1
    head = 2 * pl.program_id(0) + pl.program_id(1)
    qc = lax.broadcasted_iota(jnp.int32, (GRID_W, GRID_W), 0)
    kc = lax.broadcasted_iota(jnp.int32, (GRID_W, GRID_W), 1)
    cstart = jnp.clip(qc - NA_WIN_COLS // 2, 0, GRID_W - NA_WIN_COLS)
    col_ok = (kc >= cstart) & (kc < cstart + NA_WIN_COLS)
    dcol = jnp.clip(kc - qc + NA_WIN_COLS - 1, 0, n_dc - 1)
    base = head * (n_dr * n_dc)
    tiles = []
    for dr in range(n_dr):
        acc = jnp.zeros((GRID_W, GRID_W), F32)
        for dc in range(n_dc):
            acc = jnp.where(dcol == dc, rpb_ref[base + dr * n_dc + dc], acc)
        tiles.append(jnp.where(col_ok, acc, NEG_INF))
    for var in range(NA_WIN_ROWS):
        for krow in range(NA_WIN_ROWS):
            o_ref[0, var, 0, :, krow * GRID_W:(krow + 1) * GRID_W] = tiles[var + krow]


def _na_bias_table(rpb):
    shape = (NA_HEADS // 2, NA_WIN_ROWS, 2, GRID_W, NA_WIN_ROWS * GRID_W)
    return pl.pallas_call(
        _na_bias_kernel,
        grid=(NA_HEADS // 2, 2),
        in_specs=[pl.BlockSpec(memory_space=pltpu.SMEM)],
        out_specs=pl.BlockSpec((1, NA_WIN_ROWS, 1, GRID_W, NA_WIN_ROWS * GRID_W), lambda p, hh: (p, 0, hh, 0, 0)),
        out_shape=jax.ShapeDtypeStruct(shape, F32),
        compiler_params=_cparams("parallel", "parallel"),
        name="na_bias_table",
    )(rpb.reshape(-1).astype(F32))


def _gla_constants():
    c = GLA_CHUNK
    t = jnp.arange(c)[:, None]
    s = jnp.arange(c)[None, :]
    mats = []
    for direction in range(2):
        rows = [(s <= t) if direction == 0 else (s >= t), jnp.ones((c, c), bool)]
        for lvl in range(GLA_LEVELS):
            half = 1 << lvl
            blk_start = (t >> (lvl + 1)) << (lvl + 1)
            if direction == 0:
                rows.append(s <= blk_start + half - 1)
            else:
                rows.append(s >= blk_start + half)
        mats.append(jnp.concatenate(rows, axis=0))
    return jnp.stack(mats).astype(BF16)


def _hgrn_kernel(zq_ref, zi_ref, zf_ref, lbl_ref, cm_ref, o_ref, state_ref, *, layer):
    direction = pl.program_id(1)
    c = GLA_CHUNK
    dk = LANES

    @pl.when(pl.program_id(2) == 0)
    def _():
        state_ref[...] = jnp.zeros_like(state_ref)

    logits = lbl_ref[...]
    e = jnp.exp(logits - logits.max(axis=0, keepdims=True))
    lb = e[:layer + 1].sum(axis=0, keepdims=True) / e.sum(axis=0, keepdims=True)

    f = lb + (1.0 - lb) * _sigmoid(zf_ref[...])
    logf = jnp.log(f)
    hi = logf.astype(BF16)
    r1 = logf - hi.astype(F32)
    mid = r1.astype(BF16)
    lo = (r1 - mid.astype(F32)).astype(BF16)
    cm = cm_ref[0]
    sums = _nn(cm, hi) + _nn(cm, mid) + _nn(cm, lo)

    kk = 1.0 - f
    qq = _silu(zq_ref[...])
    vv = zi_ref[...]

    row = lax.broadcasted_iota(jnp.int32, (c, dk), 0)
    trow = lax.broadcasted_iota(jnp.int32, (c, c), 0)
    scol = lax.broadcasted_iota(jnp.int32, (c, c), 1)
    q_half = 1 - direction

    for h in range(HGRN_HEADS):
        sl = slice(h * dk, (h + 1) * dk)
        b = sums[0:c, sl]
        tot = sums[c:2 * c, sl]
        q, k, v = qq[:, sl], kk[:, sl], vv[:, sl]
        vb = v.astype(BF16)
        att = jnp.zeros((c, c), F32)
        for lvl in range(GLA_LEVELS):
            ref = sums[(2 + lvl) * c:(3 + lvl) * c, sl]
            is_q = ((row >> lvl) & 1) == q_half
            qh = jnp.where(is_q, jnp.exp(jnp.where(is_q, b - ref, 0.0)), 0.0) * q
            kh = jnp.where(is_q, 0.0, jnp.exp(jnp.where(is_q, 0.0, ref - b))) * k
            same = (trow >> (lvl + 1)) == (scol >> (lvl + 1))
            att = att + jnp.where(same, _nt(qh.astype(BF16), kh.astype(BF16)), 0.0)
        o = _nn(att.astype(BF16), vb) + jnp.sum(q * k, axis=-1, keepdims=True) * v
        state = state_ref[h]
        o = o + _nn((q * jnp.exp(b)).astype(BF16), state.astype(BF16))
        kt = (k * jnp.exp(tot - b)).T.astype(BF16)
        state_ref[h] = jnp.exp(tot[0:1, :]).T * state + _nn(kt, vb)
        o_ref[0, :, sl] = o


def _hgrn_core(z, lb_logits, layer, n_batch, seq, n_ctx):
    t_all, n5 = z.shape
    d = n5 // 5
    c = GLA_CHUNK
    n_chunks_x, n_chunks_c = seq // c, n_ctx // c
    n_chunks = n_chunks_x + n_chunks_c
    ctx_base = n_batch * n_chunks_x

    def row_block(b, direction, ci):
        fwd = jnp.where(ci < n_chunks_c, ctx_base + b * n_chunks_c + ci, b * n_chunks_x + ci - n_chunks_c)
        cj = n_chunks - 1 - ci
        bwd = jnp.where(cj < n_chunks_x, b * n_chunks_x + cj, ctx_base + b * n_chunks_c + cj - n_chunks_x)
        return jnp.where(direction == 0, fwd, bwd)

    n_lb = lb_logits.shape[0]
    return pl.pallas_call(
        functools.partial(_hgrn_kernel, layer=layer),
        grid=(n_batch, 2, n_chunks),
        in_specs=[pl.BlockSpec((c, d), lambda b, dr, ci: (row_block(b, dr, ci), 0)),
                  pl.BlockSpec((c, d), lambda b, dr, ci: (row_block(b, dr, ci), 1)),
                  pl.BlockSpec((c, d), lambda b, dr, ci: (row_block(b, dr, ci), 2 + dr)),
                  pl.BlockSpec((n_lb, d), lambda b, dr, ci: (0, 0)),
                  pl.BlockSpec((1, 8 * c, c), lambda b, dr, ci: (dr, 0, 0))],
        out_specs=pl.BlockSpec((1, c, d), lambda b, dr, ci: (dr, row_block(b, dr, ci), 0)),
        out_shape=jax.ShapeDtypeStruct((2, t_all, d), F32),
        scratch_shapes=[pltpu.VMEM((HGRN_HEADS, LANES, LANES), F32)],
        compiler_params=_cparams("parallel", "parallel", "arbitrary"),
        name="hgrn_gla",
    )(z, z, z, lb_logits, _gla_constants())


def _out_attn_kernel(a_ref, w_ref, x_ref, mod_ref, g_ref, b_ref, o_ref, *, alpha):
    y = _nn(a_ref[...], w_ref[...])
    u = alpha * x_ref[...] + mod_ref[0, 2:3, :] * y
    o_ref[...] = _layer_norm(u, g_ref[...], b_ref[...])


def _out_hgrn_kernel(of_ref, ob_ref, zg_ref, ng_ref, w_ref, x_ref, mod_ref, g_ref, b_ref, o_ref, *, alpha):
    o = of_ref[0] + ob_ref[0]
    gate = _silu(zg_ref[...])
    parts = []
    for h in range(HGRN_HEADS):
        sl = slice(h * LANES, (h + 1) * LANES)
        parts.append((_rms_block(o[:, sl], ng_ref[...]) * gate[:, sl]).astype(BF16))
    y = _nn(jnp.concatenate(parts, axis=1), w_ref[...])
    u = alpha * x_ref[...] + mod_ref[0, 2:3, :] * y
    o_ref[...] = _layer_norm(u, g_ref[...], b_ref[...])


def _top16_rows(s, n_rows):
    iota = lax.broadcasted_iota(jnp.int32, s.shape, 0).astype(F32)
    rank = jnp.full(s.shape, float(PEER_TOPK), F32)
    vals = []
    for a in range(PEER_TOPK):
        m = s.max(axis=0, keepdims=True)
        first = jnp.where(s == m, iota, float(n_rows)).min(axis=0, keepdims=True)
        sel = iota == first
        rank = jnp.where(sel, float(a), rank)
        s = jnp.where(sel, -jnp.inf, s)
        vals.append(m)
    return vals, rank


def _top16_values(s):
    vals = []
    for _ in range(PEER_TOPK):
        m = s.max(axis=0, keepdims=True)
        s = jnp.where(s == m, -jnp.inf, s)
        vals.append(m)
    covered = jnp.where(s == -jnp.inf, 1.0, 0.0).sum(axis=0, keepdims=True)
    return vals, covered


def _pack_rows(rows):
    n, width = len(rows), rows[0].shape[1]
    idx = lax.broadcasted_iota(jnp.int32, (n, width), 0)
    out = jnp.zeros((n, width), F32)
    for r, row in enumerate(rows):
        out = jnp.where(idx == r, row, out)
    return out


def _select_fast(s0, s1):
    k = PEER_TOPK
    width = s0.shape[1]
    v0, cov0 = _top16_values(s0)
    v1, cov1 = _top16_values(s1)
    p0, p1 = _pack_rows(v0), _pack_rows(v1)
    row8 = lax.broadcasted_iota(jnp.int32, (8, width), 0)
    blocks = [v0[0] + p1]
    for a in range(1, 8):
        blocks.append(jnp.where(row8 < k // (a + 1), v0[a] + p1[0:8], -jnp.inf))
    blocks.append(p0[8:16] + v1[0])
    cand = jnp.concatenate(blocks, axis=0)
    cvals, _ = _top16_values(cand)
    chosen = cand >= cvals[k - 1]
    n_chosen = jnp.where(chosen, 1.0, 0.0).sum(axis=0, keepdims=True)
    ok = jnp.where((cov0 == float(k)) & (cov1 == float(k)) & (n_chosen == float(k)), 1.0, 0.0)
    cmax = v0[0] + v1[0]
    zsum = jnp.where(chosen, jnp.exp(cand - cmax), 0.0).sum(axis=0, keepdims=True)
    theta = jnp.full(s0.shape, jnp.inf, F32)
    off = 0
    for a in range(8):
        rows = k if a == 0 else 8
        th_a = jnp.where(chosen[off:off + rows], p1[0:rows], jnp.inf).min(axis=0, keepdims=True)
        theta = jnp.where(s0 == v0[a], th_a, theta)
        off += rows
    for a in range(8, k):
        th_a = jnp.where(chosen[off + a - 8:off + a - 7], v1[0], jnp.inf)
        theta = jnp.where(s0 == v0[a], th_a, theta)
    return jnp.exp(s0 - v0[0]) / zsum, theta, jnp.exp(s1 - v1[0]), s1, ok


def _select_exact(s0, s1):
    k = PEER_TOPK
    v0, rank0 = _top16_rows(s0, s0.shape[0])
    v1, rank1 = _top16_rows(s1, s1.shape[0])
    p1 = _pack_rows(v1)
    cand = jnp.concatenate([v0[a] + p1 for a in range(k)], axis=0)
    _, crank = _top16_rows(cand, k * k)
    chosen = crank < float(k)
    cmax = v0[0] + v1[0]
    zsum = jnp.where(chosen, jnp.exp(cand - cmax), 0.0).sum(axis=0, keepdims=True)
    theta = jnp.full(s0.shape, jnp.inf, F32)
    for a in range(k):
        n_a = jnp.where(chosen[a * k:(a + 1) * k], 1.0, 0.0).sum(axis=0, keepdims=True)
        theta = jnp.where(rank0 == float(a), jnp.where(n_a > 0.0, 1.0 - n_a, jnp.inf), theta)
    return jnp.exp(s0 - v0[0]) / zsum, theta, jnp.exp(s1 - v1[0]), -rank1


def _peer_select_kernel(x_ref, mod_ref, wq_ref, keys_ref, a_ref, th_ref, b_ref, s_ref, qt_ref):
    tok = _modulate(x_ref, mod_ref, 3).astype(BF16)
    qt_ref[...] = _nt(wq_ref[...], tok).astype(BF16)
    tt = tok.shape[0]
    nk = PEER_NKEYS
    chunks = [slice(lc * LANES, (lc + 1) * LANES) for lc in range(tt // LANES)]

    def scores(hp):
        return _nn(keys_ref[hp], qt_ref[pl.ds(pl.multiple_of(hp * nk, nk), nk), :])

    def fast_head(h, ok):
        s0, s1 = scores(2 * h), scores(2 * h + 1)
        oks = []
        for ls in chunks:
            a, th, b, s, ok_c = _select_fast(s0[:, ls], s1[:, ls])
            a_ref[h, :, ls], th_ref[h, :, ls], b_ref[h, :, ls], s_ref[h, :, ls] = a, th, b, s
            oks.append(ok_c)
        return jnp.minimum(ok, jnp.concatenate(oks, axis=1))

    ok = lax.fori_loop(0, PEER_HEADS, fast_head, jnp.ones((1, tt), F32))

    @pl.when(jnp.min(ok) < 0.5)
    def _():
        def exact_head(h, carry):
            s0, s1 = scores(2 * h), scores(2 * h + 1)
            for ls in chunks:
                a, th, b, s = _select_exact(s0[:, ls], s1[:, ls])
                a_ref[h, :, ls], th_ref[h, :, ls], b_ref[h, :, ls], s_ref[h, :, ls] = a, th, b, s
            return carry
        lax.fori_loop(0, PEER_HEADS, exact_head, 0)


def _gelu_tanh(x):
    return 0.5 * x * (1.0 + jnp.tanh(0.7978845608028654 * (x + 0.044715 * (x * x * x))))


def _peer_dense_kernel(x_ref, mod_ref, a_ref, th_ref, b_ref, s_ref, u_ref, vt_ref, g_ref, beta_ref, o_ref,
                       tok_ref, acc_ref, act_ref, p_ref, *, alpha):
    e = pl.program_id(1)
    nk = PEER_NKEYS
    tt = tok_ref.shape[0]

    @pl.when(e == 0)
    def _():
        tok_ref[...] = _modulate(x_ref, mod_ref, 3).astype(BF16)
        acc_ref[...] = jnp.zeros_like(acc_ref)

    act_ref[...] = _nt(u_ref[...], tok_ref[...])
    i0 = pl.multiple_of(e * PEER_IBLK, PEER_IBLK)
    for lc in range(tt // LANES):
        ls = slice(lc * LANES, (lc + 1) * LANES)
        for jc in range(nk // PEER_JCHUNK):
            js = slice(jc * PEER_JCHUNK, (jc + 1) * PEER_JCHUNK)
            w = [None] * PEER_IBLK
            for h in range(PEER_HEADS):
                sv, bv = s_ref[h, js, ls], b_ref[h, js, ls]
                a_rows = a_ref[h, pl.ds(i0, PEER_IBLK), ls]
                t_rows = th_ref[h, pl.ds(i0, PEER_IBLK), ls]
                for ii in range(PEER_IBLK):
                    a_row, t_row = a_rows[ii:ii + 1, :], t_rows[ii:ii + 1, :]
                    term = a_row * jnp.where(sv >= t_row, bv, 0.0)
                    w[ii] = term if w[ii] is None else w[ii] + term
            for ii in range(PEER_IBLK):
                rows = slice(ii * nk + jc * PEER_JCHUNK, ii * nk + (jc + 1) * PEER_JCHUNK)
                p_ref[rows, ls] = (w[ii] * _gelu_tanh(act_ref[rows, ls])).astype(BF16)
    acc_ref[...] += _nn(vt_ref[...], p_ref[...])

    @pl.when(e == pl.num_programs(1) - 1)
    def _():
        y = acc_ref[...].T
        u = alpha * x_ref[...] + mod_ref[0, 5:6, :] * y
        o_ref[...] = _layer_norm(u, g_ref[...], beta_ref[...])


def _peer(x1, mod_l, wq_t, keys, u_bf, vt_bf, ln_g, ln_b, alpha, seq, n_batch, n_rows):
    d = x1.shape[1]
    tt = PEER_TOK_TILE
    nk = PEER_NKEYS
    n_tiles = n_rows // tt
    mod_map = _mod_row_map(tt, seq, n_batch)
    sel_shape = jax.ShapeDtypeStruct((PEER_HEADS, nk, n_rows), F32)
    sel_spec = pl.BlockSpec((PEER_HEADS, nk, tt), lambda i: (0, 0, i))
    a, th, b, sc = pl.pallas_call(
        _peer_select_kernel,
        grid=(n_tiles,),
        in_specs=[pl.BlockSpec((tt, d), lambda i: (i, 0)),
                  pl.BlockSpec((1, 6, d), mod_map),
                  pl.BlockSpec(wq_t.shape, lambda i: (0, 0)),
                  pl.BlockSpec(keys.shape, lambda i: (0, 0, 0))],
        out_specs=[sel_spec] * 4,
        out_shape=[sel_shape] * 4,
        scratch_shapes=[pltpu.VMEM((wq_t.shape[0], tt), BF16)],
        compiler_params=_cparams("parallel"),
        name="peer_select",
    )(x1, mod_l, wq_t, keys)

    et = PEER_IBLK * nk
    n_eblk = u_bf.shape[0] // et
    sel_spec2 = pl.BlockSpec((PEER_HEADS, nk, tt), lambda i, e: (0, 0, i))
    return pl.pallas_call(
        functools.partial(_peer_dense_kernel, alpha=alpha),
        grid=(n_tiles, n_eblk),
        in_specs=[pl.BlockSpec((tt, d), lambda i, e: (i, 0)),
                  pl.BlockSpec((1, 6, d), lambda i, e: mod_map(i)),
                  sel_spec2, sel_spec2, sel_spec2, sel_spec2,
                  pl.BlockSpec((et, d), lambda i, e: (e, 0)),
                  pl.BlockSpec((d, et), lambda i, e: (0, e)),
                  pl.BlockSpec((1, d), lambda i, e: (0, 0)),
                  pl.BlockSpec((1, d), lambda i, e: (0, 0))],
        out_specs=pl.BlockSpec((tt, d), lambda i, e: (i, 0)),
        out_shape=jax.ShapeDtypeStruct((n_rows, d), F32),
        scratch_shapes=[pltpu.VMEM((tt, d), BF16), pltpu.VMEM((d, tt), F32), pltpu.VMEM((et, tt), F32),
                        pltpu.VMEM((et, tt), BF16)],
        compiler_params=_cparams("parallel", "arbitrary"),
        name="peer_dense",
    )(x1, mod_l, a, th, b, sc, u_bf, vt_bf, ln_g, ln_b)


def _attention_outputs(kind, q, k, v, params, n_batch, seq, n_ctx, with_ctx):
    t_lat = n_batch * seq
    t_all, d = q.shape
    ctx_blk = t_lat // n_ctx
    out_rows = t_all if with_ctx else t_lat
    out_shape = jax.ShapeDtypeStruct((out_rows, d), BF16)
    smem = pl.BlockSpec(memory_space=pltpu.SMEM)
    if kind == 1:
        sink = params
        n_q = seq // SWA_QBLK
        qw = 2 * LANES
        out = pl.pallas_call(
            functools.partial(_swa_x_kernel, seq=seq),
            grid=(n_batch, SWA_KV_HEADS, n_q),
            in_specs=[smem,
                      pl.BlockSpec((SWA_QBLK, qw), lambda b, g, n: (b * n_q + n, g)),
                      pl.BlockSpec((seq, LANES), lambda b, g, n: (b, g)),
                      pl.BlockSpec((seq, LANES), lambda b, g, n: (b, g)),
                      pl.BlockSpec((n_ctx, LANES), lambda b, g, n: (ctx_blk + b, g)),
                      pl.BlockSpec((n_ctx, LANES), lambda b, g, n: (ctx_blk + b, g))],
            out_specs=pl.BlockSpec((SWA_QBLK, qw), lambda b, g, n: (b * n_q + n, g)),
            out_shape=out_shape,
            compiler_params=_cparams("parallel", "parallel", "arbitrary"),
            name="swa_latent",
        )(sink, q, k, v, k, v)
        if with_ctx:
            out = pl.pallas_call(
                _swa_c_kernel,
                grid=(n_batch, SWA_KV_HEADS),
                in_specs=[smem,
                          pl.BlockSpec((n_ctx, qw), lambda b, g: (ctx_blk + b, g)),
                          pl.BlockSpec((n_ctx, LANES), lambda b, g: (ctx_blk + b, g)),
                          pl.BlockSpec((n_ctx, LANES), lambda b, g: (ctx_blk + b, g)),
                          pl.BlockSpec(memory_space=pl.ANY)],
                out_specs=pl.BlockSpec((n_ctx, qw), lambda b, g: (ctx_blk + b, g)),
                out_shape=out_shape,
                input_output_aliases={4: 0},
                compiler_params=_cparams("parallel", "parallel"),
                name="swa_context",
            )(sink, q, k, v, out)
        return out
    if kind == 2:
        bias = params
        n_pairs = NA_HEADS // 2
        n_grid_rows = seq // GRID_W
        rows_step = NA_ROWS_PER_STEP * GRID_W
        n_q = seq // rows_step
        out = pl.pallas_call(
            functools.partial(_na_x_kernel, n_grid_rows=n_grid_rows),
            grid=(n_batch, n_pairs, n_q),
            in_specs=[pl.BlockSpec((rows_step, LANES), lambda b, p, n: (b * n_q + n, p)),
                      pl.BlockSpec((seq, LANES), lambda b, p, n: (b, p)),
                      pl.BlockSpec((seq, LANES), lambda b, p, n: (b, p)),
                      pl.BlockSpec((n_ctx, LANES), lambda b, p, n: (ctx_blk + b, p)),
                      pl.BlockSpec((n_ctx, LANES), lambda b, p, n: (ctx_blk + b, p)),
                      pl.BlockSpec((1,) + bias.shape[1:], lambda b, p, n: (p, 0, 0, 0, 0))],
            out_specs=pl.BlockSpec((rows_step, LANES), lambda b, p, n: (b * n_q + n, p)),
            out_shape=out_shape,
            compiler_params=_cparams("parallel", "parallel", "arbitrary"),
            name="na_latent",
        )(q, k, v, k, v, bias)
        if with_ctx:
            out = pl.pallas_call(
                _na_c_kernel,
                grid=(n_batch, n_pairs),
                in_specs=[pl.BlockSpec((n_ctx, LANES), lambda b, p: (ctx_blk + b, p)),
                          pl.BlockSpec((n_ctx, LANES), lambda b, p: (ctx_blk + b, p)),
                          pl.BlockSpec((n_ctx, LANES), lambda b, p: (ctx_blk + b, p)),
                          pl.BlockSpec(memory_space=pl.ANY)],
                out_specs=pl.BlockSpec((n_ctx, LANES), lambda b, p: (ctx_blk + b, p)),
                out_shape=out_shape,
                input_output_aliases={3: 0},
                compiler_params=_cparams("parallel", "parallel"),
                name="na_context",
            )(q, k, v, out)
        return out
    assert kind == 3 and not with_ctx
    n_q = seq // GQA_QBLK
    qw = (GQA_Q_HEADS // GQA_KV_HEADS) * LANES
    return pl.pallas_call(
        _gqa_x_kernel,
        grid=(n_batch, GQA_KV_HEADS, n_q),
        in_specs=[pl.BlockSpec((GQA_QBLK, qw), lambda b, g, n: (b * n_q + n, g)),
                  pl.BlockSpec((seq, LANES), lambda b, g, n: (b, g)),
                  pl.BlockSpec((seq, LANES), lambda b, g, n: (b, g)),
                  pl.BlockSpec((n_ctx, LANES), lambda b, g, n: (ctx_blk + b, g)),
                  pl.BlockSpec((n_ctx, LANES), lambda b, g, n: (ctx_blk + b, g))],
        out_specs=pl.BlockSpec((GQA_QBLK, qw), lambda b, g, n: (b * n_q + n, g)),
        out_shape=out_shape,
        compiler_params=_cparams("parallel", "parallel", "arbitrary"),
        name="gqa_latent",
    )(q, k, v, k, v)


def _out_proj(kernel, lead_args, lead_specs, w_out, x, mod_l, ln_g, ln_b, tm, seq, n_batch, n_rows):
    d = x.shape[1]
    row = lambda i: (i, 0)
    const = lambda i: (0, 0)
    return pl.pallas_call(
        kernel,
        grid=(n_rows // tm,),
        in_specs=lead_specs + [pl.BlockSpec((d, d), const),
                               pl.BlockSpec((tm, d), row),
                               pl.BlockSpec((1, 6, d), _mod_row_map(tm, seq, n_batch)),
                               pl.BlockSpec((1, d), const),
                               pl.BlockSpec((1, d), const)],
        out_specs=pl.BlockSpec((tm, d), row),
        out_shape=jax.ShapeDtypeStruct((n_rows, d), F32),
        compiler_params=_cparams("parallel"),
        name="out_proj_ln",
    )(*lead_args, w_out, x, mod_l, ln_g, ln_b)


def kernel(x, c, ctx, c_ctx, mod_w, mod_b, ln_g, ln_b, peer_wq, peer_keys, peer_u, peer_v, hgrn_w_in, hgrn_lb_logits, hgrn_norm_g, hgrn_w_out, swa_w_in, swa_sink, swa_w_out, na_w_in, na_rpb, na_w_out, gqa_w_in, gqa_q_norm, gqa_k_norm, gqa_w_out):
    n_batch, seq, d = x.shape
    n_ctx = ctx.shape[1]
    depth = mod_w.shape[0]
    alpha = (2 * depth) ** 0.25
    t_lat = n_batch * seq
    t_ctx = n_batch * n_ctx
    assert n_batch < 16 and seq % (NA_WIN_ROWS * GRID_W) == 0 and seq % n_ctx == 0
    assert (t_lat % PEER_TOK_TILE == 0) and (t_ctx % PEER_TOK_TILE == 0) and seq % PEER_TOK_TILE == 0

    xs = jnp.concatenate([x.reshape(t_lat, d), ctx.reshape(t_ctx, d)], axis=0)
    cond = jnp.zeros((16, d), F32).at[:n_batch].set(c).at[n_batch].set(c_ctx)
    mod = _modulation(cond, mod_w, mod_b).reshape(depth, 16, 6, d)

    tm = _row_tile(seq, t_ctx, 512)
    row = lambda i: (i, 0)

    for i in range(depth):
        kind, occ = i % 4, i // 4
        last = i == depth - 1
        n_rows = t_lat if last else t_lat + t_ctx
        mod_l = mod[i]
        g1, b1 = ln_g[i, 0][None], ln_b[i, 0][None]
        g2, b2 = ln_g[i, 1][None], ln_b[i, 1][None]
        if kind == 0:
            tmh = _row_tile(seq, t_ctx, 256)
            z = _in_proj(_in_hgrn_kernel, xs, mod_l, hgrn_w_in[occ].astype(BF16), [], [], [5 * d], F32,
                         tmh, seq, n_batch)
            o2 = _hgrn_core(z, hgrn_lb_logits, i, n_batch, seq, n_ctx)
            lead_specs = [pl.BlockSpec((1, tmh, d), lambda j: (0, j, 0)),
                          pl.BlockSpec((1, tmh, d), lambda j: (1, j, 0)),
                          pl.BlockSpec((tmh, d), lambda j: (j, 4)),
                          pl.BlockSpec((1, LANES), lambda j: (0, 0))]
            x1 = _out_proj(functools.partial(_out_hgrn_kernel, alpha=alpha),
                           [o2, o2, z, hgrn_norm_g[occ][None]], lead_specs, hgrn_w_out[occ].astype(BF16),
                           xs, mod_l, g1, b1, tmh, seq, n_batch, n_rows)
        else:
            if kind == 1:
                cos_t, sin_t = _rope_tables(seq, 64, tm)
                q, k, v = _in_proj(_in_swa_kernel, xs, mod_l, swa_w_in[occ].astype(BF16), [cos_t, sin_t],
                                   _rope_specs(tm, seq, t_lat), [d, 2 * SWA_KV_HEADS * 64, 2 * SWA_KV_HEADS * 64], BF16,
                                   tm, seq, n_batch)
                params, w_out = swa_sink[occ], swa_w_out[occ]
            elif kind == 2:
                q, k, v = _in_proj(_in_na_kernel, xs, mod_l, na_w_in[occ].astype(BF16), [], [], [d, d, d], BF16,
                                   tm, seq, n_batch)
                params, w_out = _na_bias_table(na_rpb[occ]), na_w_out[occ]
            else:
                cos_t, sin_t = _rope_tables(seq, LANES, tm)
                norm_spec = pl.BlockSpec((1, LANES), lambda j: (0, 0))
                q, k, v = _in_proj(_in_gqa_kernel, xs, mod_l, gqa_w_in[occ].astype(BF16),
                                   [cos_t, sin_t, gqa_q_norm[occ][None], gqa_k_norm[occ][None]],
                                   _rope_specs(tm, seq, t_lat) + [norm_spec, norm_spec],
                                   [d, GQA_KV_HEADS * LANES, GQA_KV_HEADS * LANES], BF16, tm, seq, n_batch)
                params, w_out = None, gqa_w_out[occ]
            att = _attention_outputs(kind, q, k, v, params, n_batch, seq, n_ctx, not last)
            x1 = _out_proj(functools.partial(_out_attn_kernel, alpha=alpha), [att], [pl.BlockSpec((tm, d), row)],
                           w_out.astype(BF16), xs, mod_l, g1, b1, tm, seq, n_batch, n_rows)
        keys = peer_keys[i].reshape(2 * PEER_HEADS, PEER_NKEYS, -1).astype(BF16)
        xs = _peer(x1, mod_l, peer_wq[i].T.astype(BF16), keys, peer_u[i].astype(BF16),
                   peer_v[i].T.astype(BF16), g2, b2, alpha, seq, n_batch, n_rows)
    return xs[:t_lat].reshape(n_batch, seq, d)
```

```python
import functools

import jax
import jax.numpy as jnp
from jax import lax
from jax.experimental import pallas as pl
from jax.experimental.pallas import tpu as pltpu

F32 = jnp.float32
BF16 = jnp.bfloat16

LANES = 128
GRID_W = 64
LN_EPS = 1e-5
RMS_EPS = 1e-6
NEG_INF = -1e30
ROPE_THETA = 10000.0

HGRN_HEADS = 8
GLA_CHUNK = 64
GLA_LEVELS = 6

SWA_Q_HEADS = 16
SWA_KV_HEADS = 4
SWA_WINDOW = 128
SWA_QBLK = 128

NA_HEADS = 16
NA_WIN_ROWS = 8
NA_WIN_COLS = 16
NA_ROWS_PER_STEP = 4

GQA_Q_HEADS = 8
GQA_KV_HEADS = 2
GQA_QBLK = 128

PEER_HEADS = 8
PEER_NKEYS = 128
PEER_TOPK = 16
PEER_SEL_TILE = 256
PEER_TOK_TILE = 512
PEER_IBLK = 8
PEER_IGRP = 2
PEER_JCHUNK = 16

VMEM_LIMIT = 56 * 1024 * 1024


def _cparams(*sem):
    return pltpu.CompilerParams(dimension_semantics=sem, vmem_limit_bytes=VMEM_LIMIT)


def _nt(a, b):
    return lax.dot_general(a, b, (((1,), (1,)), ((), ())), preferred_element_type=F32)


def _nn(a, b):
    return jnp.dot(a, b, preferred_element_type=F32)


def _sigmoid(x):
    return 1.0 / (1.0 + jnp.exp(-x))


def _silu(x):
    return x * _sigmoid(x)


def _layer_norm(u, g, b):
    mu = jnp.mean(u, axis=-1, keepdims=True)
    d = u - mu
    var = jnp.mean(d * d, axis=-1, keepdims=True)
    return d * lax.rsqrt(var + LN_EPS) * g + b


def _row_tile(seq, n_ctx_rows, pref):
    for t in (512, 256, 128, 64):
        if t <= pref and seq % t == 0 and n_ctx_rows % t == 0:
            return t
    raise ValueError("no row tile")


def _mod_row_map(tm, seq, n_batch):
    return lambda i: (jnp.minimum((i * tm) // seq, n_batch), 0, 0)


def _mod_kernel(c_ref, w_ref, b_ref, o_ref):
    h = _silu(c_ref[...]).astype(BF16)
    o_ref[0] = _nn(h, w_ref[0].astype(BF16)) + b_ref[0]


def _modulation(cond, mod_w, mod_b):
    depth, d, n = mod_w.shape
    tn = 1536
    return pl.pallas_call(
        _mod_kernel,
        grid=(depth, n // tn),
        in_specs=[pl.BlockSpec((16, d), lambda l, j: (0, 0)),
                  pl.BlockSpec((1, d, tn), lambda l, j: (l, 0, j)),
                  pl.BlockSpec((1, 1, tn), lambda l, j: (l, 0, j))],
        out_specs=pl.BlockSpec((1, 16, tn), lambda l, j: (l, 0, j)),
        out_shape=jax.ShapeDtypeStruct((depth, 16, n), F32),
        compiler_params=_cparams("parallel", "parallel"),
        name="modulation",
    )(cond, mod_w, mod_b.reshape(depth, 1, n))


def _rope_tables(seq, dh, tm):
    t = jnp.arange(seq)
    pos = jnp.stack([t // GRID_W, t % GRID_W]).astype(F32)
    d_axis = dh // 2
    inv = ROPE_THETA ** (-jnp.arange(0, d_axis, 2, dtype=F32) / d_axis)
    ang = pos[:, :, None] * inv
    cos, sin = jnp.cos(ang), jnp.sin(ang)
    cos_h = jnp.concatenate([cos[0], cos[0], cos[1], cos[1]], axis=-1)
    sin_h = jnp.concatenate([-sin[0], sin[0], -sin[1], sin[1]], axis=-1)
    reps = LANES // dh
    cos_t = jnp.concatenate([jnp.tile(cos_h, (1, reps)), jnp.ones((tm, LANES), F32)], axis=0)
    sin_t = jnp.concatenate([jnp.tile(sin_h, (1, reps)), jnp.zeros((tm, LANES), F32)], axis=0)
    return cos_t, sin_t


def _rope_block(zb, cos, sin, quarter):
    lane = lax.broadcasted_iota(jnp.int32, zb.shape, 1)
    first = (lane % (2 * quarter)) < quarter
    up = pltpu.roll(zb, LANES - quarter, 1)
    down = pltpu.roll(zb, quarter, 1)
    return zb * cos + jnp.where(first, up, down) * sin


def _dup_half(blk, half):
    lane = lax.broadcasted_iota(jnp.int32, blk.shape, 1)
    swapped = pltpu.roll(blk, 64, 1)
    keep = (lane < 64) if half == 0 else (lane >= 64)
    return jnp.where(keep, blk, swapped)


def _modulate(x_ref, mod_ref, shift_row):
    return x_ref[...] * (1.0 + mod_ref[0, shift_row + 1:shift_row + 2, :]) + mod_ref[0, shift_row:shift_row + 1, :]


def _in_hgrn_kernel(x_ref, mod_ref, w_ref, o_ref):
    h = _modulate(x_ref, mod_ref, 0).astype(BF16)
    o_ref[...] = _nn(h, w_ref[...])


def _in_swa_kernel(x_ref, mod_ref, w_ref, cos_ref, sin_ref, q_ref, k_ref, v_ref):
    h = _modulate(x_ref, mod_ref, 0).astype(BF16)
    z = _nn(h, w_ref[...])
    cos, sin = cos_ref[...], sin_ref[...]
    dq = SWA_Q_HEADS * 64
    dkv = SWA_KV_HEADS * 64
    for blk in range(dq // LANES):
        zb = z[:, blk * LANES:(blk + 1) * LANES]
        q_ref[:, blk * LANES:(blk + 1) * LANES] = (_rope_block(zb, cos, sin, 16) * 0.125).astype(BF16)
    for blk in range(dkv // LANES):
        kb = _rope_block(z[:, dq + blk * LANES:dq + (blk + 1) * LANES], cos, sin, 16)
        vb = z[:, dq + dkv + blk * LANES:dq + dkv + (blk + 1) * LANES]
        for half in range(2):
            kv = 2 * blk + half
            k_ref[:, kv * LANES:(kv + 1) * LANES] = _dup_half(kb, half).astype(BF16)
            v_ref[:, kv * LANES:(kv + 1) * LANES] = _dup_half(vb, half).astype(BF16)


def _in_na_kernel(x_ref, mod_ref, w_ref, q_ref, k_ref, v_ref):
    h = _modulate(x_ref, mod_ref, 0).astype(BF16)
    z = _nn(h, w_ref[...])
    d = q_ref.shape[1]
    q_ref[...] = (z[:, :d] * 0.125).astype(BF16)
    k_ref[...] = z[:, d:2 * d].astype(BF16)
    v_ref[...] = z[:, 2 * d:].astype(BF16)


def _rms_block(zb, g):
    return zb * lax.rsqrt(jnp.mean(zb * zb, axis=-1, keepdims=True) + RMS_EPS) * g


def _in_gqa_kernel(x_ref, mod_ref, w_ref, cos_ref, sin_ref, qn_ref, kn_ref, q_ref, k_ref, v_ref):
    h = _modulate(x_ref, mod_ref, 0).astype(BF16)
    z = _nn(h, w_ref[...])
    cos, sin = cos_ref[...], sin_ref[...]
    dq = GQA_Q_HEADS * LANES
    dkv = GQA_KV_HEADS * LANES
    scale = LANES ** -0.5
    for hd in range(GQA_Q_HEADS):
        zb = _rms_block(z[:, hd * LANES:(hd + 1) * LANES], qn_ref[...])
        q_ref[:, hd * LANES:(hd + 1) * LANES] = (_rope_block(zb, cos, sin, 32) * scale).astype(BF16)
    for hd in range(GQA_KV_HEADS):
        zb = _rms_block(z[:, dq + hd * LANES:dq + (hd + 1) * LANES], kn_ref[...])
        k_ref[:, hd * LANES:(hd + 1) * LANES] = _rope_block(zb, cos, sin, 32).astype(BF16)
    v_ref[...] = z[:, dq + dkv:].astype(BF16)


def _in_proj(kernel, x, mod_l, w, extra, extra_specs, out_widths, out_dtype, tm, seq, n_batch, n_rows=None):
    t_all, d = x.shape
    n_rows = t_all if n_rows is None else n_rows
    n = w.shape[1]
    in_specs = [pl.BlockSpec((tm, d), lambda i: (i, 0)),
                pl.BlockSpec((1, 6, d), _mod_row_map(tm, seq, n_batch)),
                pl.BlockSpec((d, n), lambda i: (0, 0))] + extra_specs
    outs = [jax.ShapeDtypeStruct((n_rows, wd), out_dtype) for wd in out_widths]
    out_specs = [pl.BlockSpec((tm, wd), lambda i: (i, 0)) for wd in out_widths]
    single = len(outs) == 1
    res = pl.pallas_call(
        kernel,
        grid=(n_rows // tm,),
        in_specs=in_specs,
        out_specs=out_specs[0] if single else out_specs,
        out_shape=outs[0] if single else outs,
        compiler_params=_cparams("parallel"),
        name=kernel.__name__.strip("_"),
    )(x, mod_l, w, *extra)
    return res


def _rope_specs(tm, seq, n_lat):
    idx = lambda i: (jnp.where(i * tm < n_lat, (i * tm % seq) // tm, seq // tm), 0)
    return [pl.BlockSpec((tm, LANES), idx), pl.BlockSpec((tm, LANES), idx)]


def _softmax_pv(scores, values, extra_logit=None):
    m = scores[0].max(axis=-1, keepdims=True)
    for s in scores[1:]:
        m = jnp.maximum(m, s.max(axis=-1, keepdims=True))
    if extra_logit is not None:
        m = jnp.maximum(m, extra_logit)
    den = jnp.exp(extra_logit - m) if extra_logit is not None else 0.0
    acc = None
    for s, v in zip(scores, values):
        p = jnp.exp(s - m)
        den = den + p.sum(axis=-1, keepdims=True)
        o = _nn(p.astype(BF16), v)
        acc = o if acc is None else acc + o
    return acc / den


def _half_masked_rows(q, n_heads):
    m = q.shape[0]
    lane = lax.broadcasted_iota(jnp.int32, (m, LANES), 1)
    rows = []
    for j in range(n_heads):
        blk = q[:, (j // 2) * LANES:(j // 2 + 1) * LANES]
        keep = (lane < 64) if j % 2 == 0 else (lane >= 64)
        rows.append(jnp.where(keep, blk, jnp.zeros_like(blk)))
    return jnp.concatenate(rows, axis=0)


def _merge_halves(o, n_heads, m):
    lane = lax.broadcasted_iota(jnp.int32, (m, LANES), 1)
    outs = []
    for b in range(n_heads // 2):
        outs.append(jnp.where(lane < 64, o[(2 * b) * m:(2 * b + 1) * m], o[(2 * b + 1) * m:(2 * b + 2) * m]))
    return outs


def _sink_column(sink_ref, first_head, n_heads, m):
    return jnp.concatenate([jnp.full((m, 1), sink_ref[first_head + j], F32) for j in range(n_heads)], axis=0)


def _swa_x_kernel(sink_ref, q_ref, k_ref, v_ref, kc_ref, vc_ref, o_ref, *, seq):
    g, n = pl.program_id(1), pl.program_id(2)
    grp = SWA_Q_HEADS // SWA_KV_HEADS
    span = SWA_QBLK + 2 * SWA_WINDOW
    start = n * SWA_QBLK
    cstart = pl.multiple_of(jnp.clip(start - SWA_WINDOW, 0, seq - span), SWA_QBLK)
    q4 = _half_masked_rows(q_ref[...], grp)
    s_b = _nt(q4, k_ref[pl.ds(cstart, span), :])
    s_c = _nt(q4, kc_ref[...])
    qpos = start + lax.broadcasted_iota(jnp.int32, s_b.shape, 0) % SWA_QBLK
    kpos = cstart + lax.broadcasted_iota(jnp.int32, s_b.shape, 1)
    s_b = jnp.where(jnp.abs(qpos - kpos) <= SWA_WINDOW, s_b, NEG_INF)
    sk = _sink_column(sink_ref, g * grp, grp, SWA_QBLK)
    o = _softmax_pv([s_b, s_c], [v_ref[pl.ds(cstart, span), :], vc_ref[...]], sk)
    for b, ob in enumerate(_merge_halves(o, grp, SWA_QBLK)):
        o_ref[:, b * LANES:(b + 1) * LANES] = ob.astype(BF16)


def _swa_c_kernel(sink_ref, q_ref, kc_ref, vc_ref, prev_ref, o_ref):
    del prev_ref
    g = pl.program_id(1)
    grp = SWA_Q_HEADS // SWA_KV_HEADS
    m = q_ref.shape[0]
    q4 = _half_masked_rows(q_ref[...], grp)
    sk = _sink_column(sink_ref, g * grp, grp, m)
    o = _softmax_pv([_nt(q4, kc_ref[...])], [vc_ref[...]], sk)
    for b, ob in enumerate(_merge_halves(o, grp, m)):
        o_ref[:, b * LANES:(b + 1) * LANES] = ob.astype(BF16)


def _na_x_kernel(q_ref, k_ref, v_ref, kc_ref, vc_ref, bias_ref, o_ref, *, n_grid_rows):
    rblk = pl.program_id(2)
    win = NA_WIN_ROWS * GRID_W
    for rr in range(NA_ROWS_PER_STEP):
        r = rblk * NA_ROWS_PER_STEP + rr
        rstart = jnp.clip(r - NA_WIN_ROWS // 2, 0, n_grid_rows - NA_WIN_ROWS)
        var = rstart - r + NA_WIN_ROWS - 1
        kstart = pl.multiple_of(rstart * GRID_W, GRID_W)
        q2 = _half_masked_rows(q_ref[rr * GRID_W:(rr + 1) * GRID_W, :], 2)
        bias = jnp.concatenate([bias_ref[0, var, 0], bias_ref[0, var, 1]], axis=0)
        s_w = _nt(q2, k_ref[pl.ds(kstart, win), :]) + bias
        s_c = _nt(q2, kc_ref[...])
        o = _softmax_pv([s_w, s_c], [v_ref[pl.ds(kstart, win), :], vc_ref[...]])
        o_ref[rr * GRID_W:(rr + 1) * GRID_W, :] = _merge_halves(o, 2, GRID_W)[0].astype(BF16)


def _na_c_kernel(q_ref, kc_ref, vc_ref, prev_ref, o_ref):
    del prev_ref
    m = q_ref.shape[0]
    q2 = _half_masked_rows(q_ref[...], 2)
    o = _softmax_pv([_nt(q2, kc_ref[...])], [vc_ref[...]])
    o_ref[...] = _merge_halves(o, 2, m)[0].astype(BF16)


def _gqa_x_kernel(q_ref, k_ref, v_ref, kc_ref, vc_ref, o_ref):
    grp = GQA_Q_HEADS // GQA_KV_HEADS
    q = q_ref[...]
    q4 = jnp.concatenate([q[:, j * LANES:(j + 1) * LANES] for j in range(grp)], axis=0)
    o = _softmax_pv([_nt(q4, k_ref[...]), _nt(q4, kc_ref[...])], [v_ref[...], vc_ref[...]])
    for j in range(grp):
        o_ref[:, j * LANES:(j + 1) * LANES] = o[j * GQA_QBLK:(j + 1) * GQA_QBLK].astype(BF16)


def _na_bias_kernel(rpb_ref, o_ref):
    n_dr, n_dc = 2 * NA_WIN_ROWS - 1, 2 * NA_WIN_COLS - 1
    head = 2 * pl.program_id(0) + pl.program_id(1)
    qc = lax.broadcasted_iota(jnp.int32, (GRID_W, GRID_W), 0)
    kc = lax.broadcasted_iota(jnp.int32, (GRID_W, GRID_W), 1)
    cstart = jnp.clip(qc - NA_WIN_COLS // 2, 0, GRID_W - NA_WIN_COLS)
    col_ok = (kc >= cstart) & (kc < cstart + NA_WIN_COLS)
    dcol = jnp.clip(kc - qc + NA_WIN_COLS - 1, 0, n_dc - 1)
    base = head * (n_dr * n_dc)
    tiles = []
    for dr in range(n_dr):
        acc = jnp.zeros((GRID_W, GRID_W), F32)
        for dc in range(n_dc):
            acc = jnp.where(dcol == dc, rpb_ref[base + dr * n_dc + dc], acc)
        tiles.append(jnp.where(col_ok, acc, NEG_INF))
    for var in range(NA_WIN_ROWS):
        for krow in range(NA_WIN_ROWS):
            o_ref[0, var, 0, :, krow * GRID_W:(krow + 1) * GRID_W] = tiles[var + krow]


def _na_bias_table(rpb):
    shape = (NA_HEADS // 2, NA_WIN_ROWS, 2, GRID_W, NA_WIN_ROWS * GRID_W)
    return pl.pallas_call(
        _na_bias_kernel,
        grid=(NA_HEADS // 2, 2),
        in_specs=[pl.BlockSpec(memory_space=pltpu.SMEM)],
        out_specs=pl.BlockSpec((1, NA_WIN_ROWS, 1, GRID_W, NA_WIN_ROWS * GRID_W), lambda p, hh: (p, 0, hh, 0, 0)),
        out_shape=jax.ShapeDtypeStruct(shape, F32),
        compiler_params=_cparams("parallel", "parallel"),
        name="na_bias_table",
    )(rpb.reshape(-1).astype(F32))


def _gla_constants():
    c = GLA_CHUNK
    t = jnp.arange(c)[:, None]
    s = jnp.arange(c)[None, :]
    mats = []
    for direction in range(2):
        rows = [(s <= t) if direction == 0 else (s >= t), jnp.ones((c, c), bool)]
        for lvl in range(GLA_LEVELS):
            half = 1 << lvl
            blk_start = (t >> (lvl + 1)) << (lvl + 1)
            if direction == 0:
                rows.append(s <= blk_start + half - 1)
            else:
                rows.append(s >= blk_start + half)
        mats.append(jnp.concatenate(rows, axis=0))
    return jnp.stack(mats).astype(BF16)


def _hgrn_kernel(zq_ref, zi_ref, zf_ref, lbl_ref, cm_ref, o_ref, state_ref, *, layer):
    direction = pl.program_id(1)
    c = GLA_CHUNK
    dk = LANES

    @pl.when(pl.program_id(2) == 0)
    def _():
        state_ref[...] = jnp.zeros_like(state_ref)

    logits = lbl_ref[...]
    e = jnp.exp(logits - logits.max(axis=0, keepdims=True))
    lb = e[:layer + 1].sum(axis=0, keepdims=True) / e.sum(axis=0, keepdims=True)

    f = lb + (1.0 - lb) * _sigmoid(zf_ref[...])
    logf = jnp.log(f)
    hi = logf.astype(BF16)
    r1 = logf - hi.astype(F32)
    mid = r1.astype(BF16)
    lo = (r1 - mid.astype(F32)).astype(BF16)
    cm = cm_ref[0]
    sums = _nn(cm, hi) + _nn(cm, mid) + _nn(cm, lo)

    kk = 1.0 - f
    qq = _silu(zq_ref[...])
    vv = zi_ref[...]

    row = lax.broadcasted_iota(jnp.int32, (c, dk), 0)
    trow = lax.broadcasted_iota(jnp.int32, (c, c), 0)
    scol = lax.broadcasted_iota(jnp.int32, (c, c), 1)
    q_half = 1 - direction

    for h in range(HGRN_HEADS):
        sl = slice(h * dk, (h + 1) * dk)
        b = sums[0:c, sl]
        tot = sums[c:2 * c, sl]
        q, k, v = qq[:, sl], kk[:, sl], vv[:, sl]
        vb = v.astype(BF16)
        att = jnp.zeros((c, c), F32)
        for lvl in range(GLA_LEVELS):
            ref = sums[(2 + lvl) * c:(3 + lvl) * c, sl]
            is_q = ((row >> lvl) & 1) == q_half
            qh = jnp.where(is_q, jnp.exp(jnp.where(is_q, b - ref, 0.0)), 0.0) * q
            kh = jnp.where(is_q, 0.0, jnp.exp(jnp.where(is_q, 0.0, ref - b))) * k
            same = (trow >> (lvl + 1)) == (scol >> (lvl + 1))
            att = att + jnp.where(same, _nt(qh.astype(BF16), kh.astype(BF16)), 0.0)
        o = _nn(att.astype(BF16), vb) + jnp.sum(q * k, axis=-1, keepdims=True) * v
        state = state_ref[h]
        o = o + _nn((q * jnp.exp(b)).astype(BF16), state.astype(BF16))
        kt = (k * jnp.exp(tot - b)).T.astype(BF16)
        state_ref[h] = jnp.exp(tot[0:1, :]).T * state + _nn(kt, vb)
        o_ref[0, :, sl] = o


def _hgrn_core(z, lb_logits, layer, n_batch, seq, n_ctx):
    t_all, n5 = z.shape
    d = n5 // 5
    c = GLA_CHUNK
    n_chunks_x, n_chunks_c = seq // c, n_ctx // c
    n_chunks = n_chunks_x + n_chunks_c
    ctx_base = n_batch * n_chunks_x

    def row_block(b, direction, ci):
        fwd = jnp.where(ci < n_chunks_c, ctx_base + b * n_chunks_c + ci, b * n_chunks_x + ci - n_chunks_c)
        cj = n_chunks - 1 - ci
        bwd = jnp.where(cj < n_chunks_x, b * n_chunks_x + cj, ctx_base + b * n_chunks_c + cj - n_chunks_x)
        return jnp.where(direction == 0, fwd, bwd)

    n_lb = lb_logits.shape[0]
    return pl.pallas_call(
        functools.partial(_hgrn_kernel, layer=layer),
        grid=(n_batch, 2, n_chunks),
        in_specs=[pl.BlockSpec((c, d), lambda b, dr, ci: (row_block(b, dr, ci), 0)),
                  pl.BlockSpec((c, d), lambda b, dr, ci: (row_block(b, dr, ci), 1)),
                  pl.BlockSpec((c, d), lambda b, dr, ci: (row_block(b, dr, ci), 2 + dr)),
                  pl.BlockSpec((n_lb, d), lambda b, dr, ci: (0, 0)),
                  pl.BlockSpec((1, 8 * c, c), lambda b, dr, ci: (dr, 0, 0))],
        out_specs=pl.BlockSpec((1, c, d), lambda b, dr, ci: (dr, row_block(b, dr, ci), 0)),
        out_shape=jax.ShapeDtypeStruct((2, t_all, d), F32),
        scratch_shapes=[pltpu.VMEM((HGRN_HEADS, LANES, LANES), F32)],
        compiler_params=_cparams("parallel", "parallel", "arbitrary"),
        name="hgrn_gla",
    )(z, z, z, lb_logits, _gla_constants())


def _out_attn_kernel(a_ref, w_ref, x_ref, mod_ref, g_ref, b_ref, o_ref, *, alpha):
    y = _nn(a_ref[...], w_ref[...])
    u = alpha * x_ref[...] + mod_ref[0, 2:3, :] * y
    o_ref[...] = _layer_norm(u, g_ref[...], b_ref[...])


def _out_hgrn_kernel(of_ref, ob_ref, zg_ref, ng_ref, w_ref, x_ref, mod_ref, g_ref, b_ref, o_ref, *, alpha):
    o = of_ref[0] + ob_ref[0]
    gate = _silu(zg_ref[...])
    parts = []
    for h in range(HGRN_HEADS):
        sl = slice(h * LANES, (h + 1) * LANES)
        parts.append((_rms_block(o[:, sl], ng_ref[...]) * gate[:, sl]).astype(BF16))
    y = _nn(jnp.concatenate(parts, axis=1), w_ref[...])
    u = alpha * x_ref[...] + mod_ref[0, 2:3, :] * y
    o_ref[...] = _layer_norm(u, g_ref[...], b_ref[...])


def _top16_rows(s, n_rows):
    iota = lax.broadcasted_iota(jnp.int32, s.shape, 0).astype(F32)
    rank = jnp.full(s.shape, float(PEER_TOPK), F32)
    vals = []
    for a in range(PEER_TOPK):
        m = s.max(axis=0, keepdims=True)
        first = jnp.where(s == m, iota, float(n_rows)).min(axis=0, keepdims=True)
        sel = iota == first
        rank = jnp.where(sel, float(a), rank)
        s = jnp.where(sel, -jnp.inf, s)
        vals.append(m)
    return vals, rank


def _top16_values(s, with_rank=False):
    vals = []
    rank = jnp.full(s.shape, float(PEER_TOPK), F32) if with_rank else None
    for a in range(PEER_TOPK):
        m = s.max(axis=0, keepdims=True)
        hit = s == m
        if with_rank:
            rank = jnp.where(hit, float(a), rank)
        s = jnp.where(hit, -jnp.inf, s)
        vals.append(m)
    covered = jnp.where(s == -jnp.inf, 1.0, 0.0).sum(axis=0, keepdims=True)
    return vals, covered, rank


def _pack_rows(rows):
    n, width = len(rows), rows[0].shape[1]
    idx = lax.broadcasted_iota(jnp.int32, (n, width), 0)
    out = jnp.zeros((n, width), F32)
    for r, row in enumerate(rows):
        out = jnp.where(idx == r, row, out)
    return out


def _select_fast(s0, s1):
    k = PEER_TOPK
    width = s0.shape[1]
    v0, cov0, _ = _top16_values(s0)
    v1, cov1, rank1 = _top16_values(s1, with_rank=True)
    p0, p1 = _pack_rows(v0), _pack_rows(v1)
    row8 = lax.broadcasted_iota(jnp.int32, (8, width), 0)
    blocks = [v0[0] + p1]
    for a in range(1, 8):
        blocks.append(jnp.where(row8 < k // (a + 1), v0[a] + p1[0:8], -jnp.inf))
    blocks.append(p0[8:16] + v1[0])
    cand = jnp.concatenate(blocks, axis=0)
    cvals, _, _ = _top16_values(cand)
    chosen = jnp.where(cand >= cvals[k - 1], 1.0, 0.0)
    n_chosen = chosen.sum(axis=0, keepdims=True)
    ok = jnp.where((cov0 == float(k)) & (cov1 == float(k)) & (n_chosen == float(k)), 1.0, 0.0)
    cmax = v0[0] + v1[0]
    zsum = (chosen * jnp.exp(cand - cmax)).sum(axis=0, keepdims=True)
    count = jnp.zeros(s0.shape, F32)
    off = 0
    for a in range(k):
        if a < 8:
            rows = k if a == 0 else 8
            n_a = chosen[off:off + rows].sum(axis=0, keepdims=True)
            off += rows
        else:
            n_a = chosen[off + a - 8:off + a - 7]
        count = jnp.where(s0 == v0[a], n_a, count)
    return jnp.exp(s0 - v0[0]) / zsum, count, jnp.exp(s1 - v1[0]), -rank1, ok


def _select_exact(s0, s1):
    k = PEER_TOPK
    v0, rank0 = _top16_rows(s0, s0.shape[0])
    v1, rank1 = _top16_rows(s1, s1.shape[0])
    p1 = _pack_rows(v1)
    cand = jnp.concatenate([v0[a] + p1 for a in range(k)], axis=0)
    _, crank = _top16_rows(cand, k * k)
    chosen = crank < float(k)
    cmax = v0[0] + v1[0]
    zsum = jnp.where(chosen, jnp.exp(cand - cmax), 0.0).sum(axis=0, keepdims=True)
    count = jnp.zeros(s0.shape, F32)
    for a in range(k):
        n_a = jnp.where(chosen[a * k:(a + 1) * k], 1.0, 0.0).sum(axis=0, keepdims=True)
        count = jnp.where(rank0 == float(a), n_a, count)
    return jnp.exp(s0 - v0[0]) / zsum, count, jnp.exp(s1 - v1[0]), -rank1


def _peer_select_kernel(x_ref, mod_ref, wq_ref, keys_ref, a_ref, th_ref, b_ref, s_ref, qt_ref):
    tok = _modulate(x_ref, mod_ref, 3).astype(BF16)
    qt_ref[...] = _nt(wq_ref[...], tok).astype(BF16)
    tt = tok.shape[0]
    nk = PEER_NKEYS
    chunks = [slice(lc * LANES, (lc + 1) * LANES) for lc in range(tt // LANES)]

    def scores(hp):
        return _nn(keys_ref[hp], qt_ref[pl.ds(pl.multiple_of(hp * nk, nk), nk), :])

    def fast_head(h, ok):
        s0, s1 = scores(2 * h), scores(2 * h + 1)
        oks = []
        for ls in chunks:
            a, th, b, s, ok_c = _select_fast(s0[:, ls], s1[:, ls])
            a_ref[h, :, ls], th_ref[h, :, ls] = a, th
            b_ref[h, :, ls] = pltpu.bitcast(b.astype(BF16), jnp.uint32)
            s_ref[h, :, ls] = pltpu.bitcast(s.astype(BF16), jnp.uint32)
            oks.append(ok_c)
        return jnp.minimum(ok, jnp.concatenate(oks, axis=1))

    ok = lax.fori_loop(0, PEER_HEADS, fast_head, jnp.ones((1, tt), F32))

    @pl.when(jnp.min(ok) < 0.5)
    def _():
        def exact_head(h, carry):
            s0, s1 = scores(2 * h), scores(2 * h + 1)
            for ls in chunks:
                a, th, b, s = _select_exact(s0[:, ls], s1[:, ls])
                a_ref[h, :, ls], th_ref[h, :, ls] = a, th
                b_ref[h, :, ls] = pltpu.bitcast(b.astype(BF16), jnp.uint32)
                s_ref[h, :, ls] = pltpu.bitcast(s.astype(BF16), jnp.uint32)
            return carry
        lax.fori_loop(0, PEER_HEADS, exact_head, 0)


def _gelu_tanh(x):
    hx = 0.5 * x
    t = jnp.tanh(x * (0.7978845608028654 + (0.7978845608028654 * 0.044715) * (x * x)))
    return hx + hx * t


def _peer_dense_kernel(x_ref, mod_ref, a_ref, n_ref, b_ref, s_ref, u_ref, vt_ref, g_ref, beta_ref, o_ref,
                       tokt_ref, acc_ref, act_ref, p_ref, arep_ref, nrep_ref, *, alpha):
    e = pl.program_id(1)
    n_tiles = pl.num_programs(1) - 1
    nk = PEER_NKEYS
    jch = PEER_JCHUNK
    slot = e % 2

    @pl.when(e == 0)
    def _():
        tokt_ref[...] = _modulate(x_ref, mod_ref, 3).T.astype(BF16)
        acc_ref[...] = jnp.zeros_like(acc_ref)
        p_ref[1] = jnp.zeros(p_ref.shape[1:], BF16)

    acc_ref[...] += _nn(vt_ref[...], p_ref[1 - slot])
    act_ref[...] = _nn(u_ref[...], tokt_ref[...])
    i0 = pl.multiple_of(jnp.minimum(e, n_tiles - 1) * PEER_IBLK, PEER_IBLK)
    for h in range(PEER_HEADS):
        a_rows = a_ref[h, pl.ds(i0, PEER_IBLK), :]
        n_rows = n_ref[h, pl.ds(i0, PEER_IBLK), :]
        for ii in range(PEER_IBLK):
            arep_ref[h, ii] = jnp.broadcast_to(a_rows[ii:ii + 1, :], arep_ref.shape[2:]).astype(BF16)
            nrep_ref[h, ii] = jnp.broadcast_to(n_rows[ii:ii + 1, :], nrep_ref.shape[2:]).astype(BF16)
    grp = PEER_IGRP
    zero = jnp.zeros((jch, LANES), BF16)
    for lc in range(tokt_ref.shape[1] // LANES):
        ls = slice(lc * LANES, (lc + 1) * LANES)
        for ig in range(PEER_IBLK // grp):
            w = [[None] * grp for _ in range(nk // jch)]
            for h in range(PEER_HEADS):
                reps = [(arep_ref[h, ig * grp + ii, :, ls], nrep_ref[h, ig * grp + ii, :, ls]) for ii in range(grp)]
                for jc in range(nk // jch):
                    sv = pltpu.bitcast(s_ref[h, jc * (jch // 2):(jc + 1) * (jch // 2), ls], BF16)
                    bv = pltpu.bitcast(b_ref[h, jc * (jch // 2):(jc + 1) * (jch // 2), ls], BF16)
                    for ii, (a_rep, n_rep) in enumerate(reps):
                        term = a_rep * jnp.minimum(bv, jnp.maximum(sv + n_rep, zero))
                        w[jc][ii] = term if w[jc][ii] is None else w[jc][ii] + term
            for jc in range(nk // jch):
                for ii in range(grp):
                    r0 = (ig * grp + ii) * nk + jc * jch
                    p_ref[slot, r0:r0 + jch, ls] = w[jc][ii] * _gelu_tanh(act_ref[r0:r0 + jch, ls]).astype(BF16)

    @pl.when(e == n_tiles)
    def _():
        y = acc_ref[...].T
        u = alpha * x_ref[...] + mod_ref[0, 5:6, :] * y
        o_ref[...] = _layer_norm(u, g_ref[...], beta_ref[...])


def _peer(x1, mod_l, wq_t, keys, u_bf, vt_bf, ln_g, ln_b, alpha, seq, n_batch, n_rows):
    d = x1.shape[1]
    nk = PEER_NKEYS
    tt = PEER_SEL_TILE
    sel_shape = jax.ShapeDtypeStruct((PEER_HEADS, nk, n_rows), F32)
    sel_shape_bf = jax.ShapeDtypeStruct((PEER_HEADS, nk // 2, n_rows), jnp.uint32)
    sel_spec = pl.BlockSpec((PEER_HEADS, nk, tt), lambda i: (0, 0, i))
    sel_spec_bf = pl.BlockSpec((PEER_HEADS, nk // 2, tt), lambda i: (0, 0, i))
    a, th, b, sc = pl.pallas_call(
        _peer_select_kernel,
        grid=(n_rows // tt,),
        in_specs=[pl.BlockSpec((tt, d), lambda i: (i, 0)),
                  pl.BlockSpec((1, 6, d), _mod_row_map(tt, seq, n_batch)),
                  pl.BlockSpec(wq_t.shape, lambda i: (0, 0)),
                  pl.BlockSpec(keys.shape, lambda i: (0, 0, 0))],
        out_specs=[sel_spec, sel_spec, sel_spec_bf, sel_spec_bf],
        out_shape=[sel_shape, sel_shape, sel_shape_bf, sel_shape_bf],
        scratch_shapes=[pltpu.VMEM((wq_t.shape[0], tt), BF16)],
        compiler_params=_cparams("parallel"),
        name="peer_select",
    )(x1, mod_l, wq_t, keys)

    tt = PEER_TOK_TILE
    mod_map = _mod_row_map(tt, seq, n_batch)
    et = PEER_IBLK * nk
    n_eblk = u_bf.shape[0] // et
    sel_spec2 = pl.BlockSpec((PEER_HEADS, nk, tt), lambda i, e: (0, 0, i))
    sel_spec2_bf = pl.BlockSpec((PEER_HEADS, nk // 2, tt), lambda i, e: (0, 0, i))
    return pl.pallas_call(
        functools.partial(_peer_dense_kernel, alpha=alpha),
        grid=(n_rows // tt, n_eblk + 1),
        in_specs=[pl.BlockSpec((tt, d), lambda i, e: (i, 0)),
                  pl.BlockSpec((1, 6, d), lambda i, e: mod_map(i)),
                  sel_spec2, sel_spec2, sel_spec2_bf, sel_spec2_bf,
                  pl.BlockSpec((et, d), lambda i, e: (jnp.minimum(e, n_eblk - 1), 0)),
                  pl.BlockSpec((d, et), lambda i, e: (0, jnp.maximum(e - 1, 0))),
                  pl.BlockSpec((1, d), lambda i, e: (0, 0)),
                  pl.BlockSpec((1, d), lambda i, e: (0, 0))],
        out_specs=pl.BlockSpec((tt, d), lambda i, e: (i, 0)),
        out_shape=jax.ShapeDtypeStruct((n_rows, d), F32),
        scratch_shapes=[pltpu.VMEM((d, tt), BF16), pltpu.VMEM((d, tt), F32), pltpu.VMEM((et, tt), F32),
                        pltpu.VMEM((2, et, tt), BF16),
                        pltpu.VMEM((PEER_HEADS, PEER_IBLK, PEER_JCHUNK, tt), BF16),
                        pltpu.VMEM((PEER_HEADS, PEER_IBLK, PEER_JCHUNK, tt), BF16)],
        compiler_params=_cparams("parallel", "arbitrary"),
        name="peer_dense",
    )(x1, mod_l, a, th, b, sc, u_bf, vt_bf, ln_g, ln_b)


def _attention_outputs(kind, q, k, v, params, n_batch, seq, n_ctx, with_ctx):
    t_lat = n_batch * seq
    t_all, d = q.shape
    ctx_blk = t_lat // n_ctx
    out_rows = t_all if with_ctx else t_lat
    out_shape = jax.ShapeDtypeStruct((out_rows, d), BF16)
    smem = pl.BlockSpec(memory_space=pltpu.SMEM)
    if kind == 1:
        sink = params
        n_q = seq // SWA_QBLK
        qw = 2 * LANES
        out = pl.pallas_call(
            functools.partial(_swa_x_kernel, seq=seq),
            grid=(n_batch, SWA_KV_HEADS, n_q),
            in_specs=[smem,
                      pl.BlockSpec((SWA_QBLK, qw), lambda b, g, n: (b * n_q + n, g)),
                      pl.BlockSpec((seq, LANES), lambda b, g, n: (b, g)),
                      pl.BlockSpec((seq, LANES), lambda b, g, n: (b, g)),
                      pl.BlockSpec((n_ctx, LANES), lambda b, g, n: (ctx_blk + b, g)),
                      pl.BlockSpec((n_ctx, LANES), lambda b, g, n: (ctx_blk + b, g))],
            out_specs=pl.BlockSpec((SWA_QBLK, qw), lambda b, g, n: (b * n_q + n, g)),
            out_shape=out_shape,
            compiler_params=_cparams("parallel", "parallel", "arbitrary"),
            name="swa_latent",
        )(sink, q, k, v, k, v)
        if with_ctx:
            out = pl.pallas_call(
                _swa_c_kernel,
                grid=(n_batch, SWA_KV_HEADS),
                in_specs=[smem,
                          pl.BlockSpec((n_ctx, qw), lambda b, g: (ctx_blk + b, g)),
                          pl.BlockSpec((n_ctx, LANES), lambda b, g: (ctx_blk + b, g)),
                          pl.BlockSpec((n_ctx, LANES), lambda b, g: (ctx_blk + b, g)),
                          pl.BlockSpec(memory_space=pl.ANY)],
                out_specs=pl.BlockSpec((n_ctx, qw), lambda b, g: (ctx_blk + b, g)),
                out_shape=out_shape,
                input_output_aliases={4: 0},
                compiler_params=_cparams("parallel", "parallel"),
                name="swa_context",
            )(sink, q, k, v, out)
        return out
    if kind == 2:
        bias = params
        n_pairs = NA_HEADS // 2
        n_grid_rows = seq // GRID_W
        rows_step = NA_ROWS_PER_STEP * GRID_W
        n_q = seq // rows_step
        out = pl.pallas_call(
            functools.partial(_na_x_kernel, n_grid_rows=n_grid_rows),
            grid=(n_batch, n_pairs, n_q),
            in_specs=[pl.BlockSpec((rows_step, LANES), lambda b, p, n: (b * n_q + n, p)),
                      pl.BlockSpec((seq, LANES), lambda b, p, n: (b, p)),
                      pl.BlockSpec((seq, LANES), lambda b, p, n: (b, p)),
                      pl.BlockSpec((n_ctx, LANES), lambda b, p, n: (ctx_blk + b, p)),
                      pl.BlockSpec((n_ctx, LANES), lambda b, p, n: (ctx_blk + b, p)),
                      pl.BlockSpec((1,) + bias.shape[1:], lambda b, p, n: (p, 0, 0, 0, 0))],
            out_specs=pl.BlockSpec((rows_step, LANES), lambda b, p, n: (b * n_q + n, p)),
            out_shape=out_shape,
            compiler_params=_cparams("parallel", "parallel", "arbitrary"),
            name="na_latent",
        )(q, k, v, k, v, bias)
        if with_ctx:
            out = pl.pallas_call(
                _na_c_kernel,
                grid=(n_batch, n_pairs),
                in_specs=[pl.BlockSpec((n_ctx, LANES), lambda b, p: (ctx_blk + b, p)),
                          pl.BlockSpec((n_ctx, LANES), lambda b, p: (ctx_blk + b, p)),
                          pl.BlockSpec((n_ctx, LANES), lambda b, p: (ctx_blk + b, p)),
                          pl.BlockSpec(memory_space=pl.ANY)],
                out_specs=pl.BlockSpec((n_ctx, LANES), lambda b, p: (ctx_blk + b, p)),
                out_shape=out_shape,
                input_output_aliases={3: 0},
                compiler_params=_cparams("parallel", "parallel"),
                name="na_context",
            )(q, k, v, out)
        return out
    assert kind == 3 and not with_ctx
    n_q = seq // GQA_QBLK
    qw = (GQA_Q_HEADS // GQA_KV_HEADS) * LANES
    return pl.pallas_call(
        _gqa_x_kernel,
        grid=(n_batch, GQA_KV_HEADS, n_q),
        in_specs=[pl.BlockSpec((GQA_QBLK, qw), lambda b, g, n: (b * n_q + n, g)),
                  pl.BlockSpec((seq, LANES), lambda b, g, n: (b, g)),
                  pl.BlockSpec((seq, LANES), lambda b, g, n: (b, g)),
                  pl.BlockSpec((n_ctx, LANES), lambda b, g, n: (ctx_blk + b, g)),
                  pl.BlockSpec((n_ctx, LANES), lambda b, g, n: (ctx_blk + b, g))],
        out_specs=pl.BlockSpec((GQA_QBLK, qw), lambda b, g, n: (b * n_q + n, g)),
        out_shape=out_shape,
        compiler_params=_cparams("parallel", "parallel", "arbitrary"),
        name="gqa_latent",
    )(q, k, v, k, v)


def _out_proj(kernel, lead_args, lead_specs, w_out, x, mod_l, ln_g, ln_b, tm, seq, n_batch, n_rows):
    d = x.shape[1]
    row = lambda i: (i, 0)
    const = lambda i: (0, 0)
    return pl.pallas_call(
        kernel,
        grid=(n_rows // tm,),
        in_specs=lead_specs + [pl.BlockSpec((d, d), const),
                               pl.BlockSpec((tm, d), row),
                               pl.BlockSpec((1, 6, d), _mod_row_map(tm, seq, n_batch)),
                               pl.BlockSpec((1, d), const),
                               pl.BlockSpec((1, d), const)],
        out_specs=pl.BlockSpec((tm, d), row),
        out_shape=jax.ShapeDtypeStruct((n_rows, d), F32),
        compiler_params=_cparams("parallel"),
        name="out_proj_ln",
    )(*lead_args, w_out, x, mod_l, ln_g, ln_b)


def kernel(x, c, ctx, c_ctx, mod_w, mod_b, ln_g, ln_b, peer_wq, peer_keys, peer_u, peer_v, hgrn_w_in, hgrn_lb_logits, hgrn_norm_g, hgrn_w_out, swa_w_in, swa_sink, swa_w_out, na_w_in, na_rpb, na_w_out, gqa_w_in, gqa_q_norm, gqa_k_norm, gqa_w_out):
    n_batch, seq, d = x.shape
    n_ctx = ctx.shape[1]
    depth = mod_w.shape[0]
    alpha = (2 * depth) ** 0.25
    t_lat = n_batch * seq
    t_ctx = n_batch * n_ctx
    assert n_batch < 16 and seq % (NA_WIN_ROWS * GRID_W) == 0 and seq % n_ctx == 0
    assert (t_lat % PEER_TOK_TILE == 0) and (t_ctx % PEER_TOK_TILE == 0) and seq % PEER_TOK_TILE == 0

    xs = jnp.concatenate([x.reshape(t_lat, d), ctx.reshape(t_ctx, d)], axis=0)
    cond = jnp.zeros((16, d), F32).at[:n_batch].set(c).at[n_batch].set(c_ctx)
    mod = _modulation(cond, mod_w, mod_b).reshape(depth, 16, 6, d)

    tm = _row_tile(seq, t_ctx, 512)
    row = lambda i: (i, 0)

    for i in range(depth):
        kind, occ = i % 4, i // 4
        last = i == depth - 1
        n_rows = t_lat if last else t_lat + t_ctx
        mod_l = mod[i]
        g1, b1 = ln_g[i, 0][None], ln_b[i, 0][None]
        g2, b2 = ln_g[i, 1][None], ln_b[i, 1][None]
        if kind == 0:
            tmh = _row_tile(seq, t_ctx, 256)
            z = _in_proj(_in_hgrn_kernel, xs, mod_l, hgrn_w_in[occ].astype(BF16), [], [], [5 * d], F32,
                         tmh, seq, n_batch)
            o2 = _hgrn_core(z, hgrn_lb_logits, i, n_batch, seq, n_ctx)
            lead_specs = [pl.BlockSpec((1, tmh, d), lambda j: (0, j, 0)),
                          pl.BlockSpec((1, tmh, d), lambda j: (1, j, 0)),
                          pl.BlockSpec((tmh, d), lambda j: (j, 4)),
                          pl.BlockSpec((1, LANES), lambda j: (0, 0))]
            x1 = _out_proj(functools.partial(_out_hgrn_kernel, alpha=alpha),
                           [o2, o2, z, hgrn_norm_g[occ][None]], lead_specs, hgrn_w_out[occ].astype(BF16),
                           xs, mod_l, g1, b1, tmh, seq, n_batch, n_rows)
        else:
            if kind == 1:
                cos_t, sin_t = _rope_tables(seq, 64, tm)
                q, k, v = _in_proj(_in_swa_kernel, xs, mod_l, swa_w_in[occ].astype(BF16), [cos_t, sin_t],
                                   _rope_specs(tm, seq, t_lat), [d, 2 * SWA_KV_HEADS * 64, 2 * SWA_KV_HEADS * 64], BF16,
                                   tm, seq, n_batch)
                params, w_out = swa_sink[occ], swa_w_out[occ]
            elif kind == 2:
                q, k, v = _in_proj(_in_na_kernel, xs, mod_l, na_w_in[occ].astype(BF16), [], [], [d, d, d], BF16,
                                   tm, seq, n_batch)
                params, w_out = _na_bias_table(na_rpb[occ]), na_w_out[occ]
            else:
                cos_t, sin_t = _rope_tables(seq, LANES, tm)
                norm_spec = pl.BlockSpec((1, LANES), lambda j: (0, 0))
                q, k, v = _in_proj(_in_gqa_kernel, xs, mod_l, gqa_w_in[occ].astype(BF16),
                                   [cos_t, sin_t, gqa_q_norm[occ][None], gqa_k_norm[occ][None]],
                                   _rope_specs(tm, seq, t_lat) + [norm_spec, norm_spec],
                                   [d, GQA_KV_HEADS * LANES, GQA_KV_HEADS * LANES], BF16, tm, seq, n_batch)
                params, w_out = None, gqa_w_out[occ]
            att = _attention_outputs(kind, q, k, v, params, n_batch, seq, n_ctx, not last)
            x1 = _out_proj(functools.partial(_out_attn_kernel, alpha=alpha), [att], [pl.BlockSpec((tm, d), row)],
                           w_out.astype(BF16), xs, mod_l, g1, b1, tm, seq, n_batch, n_rows)
        keys = peer_keys[i].reshape(2 * PEER_HEADS, PEER_NKEYS, -1).astype(BF16)
        xs = _peer(x1, mod_l, peer_wq[i].T.astype(BF16), keys, peer_u[i].astype(BF16),
                   peer_v[i].T.astype(BF16), g2, b2, alpha, seq, n_batch, n_rows)
    return xs[:t_lat].reshape(n_batch, seq, d)
```
